```python
import math
import jax, jax.numpy as jnp
from jax import lax
import numpy as np

D_MODEL = 2048
BATCH = 16
SEQ = 2048
DEPTH = 4

CTX_LEN = 256
GRID_W = 64
N_MIXERS = 3
N_RET_LAYERS = (DEPTH + 2) // 3
N_GDN_LAYERS = (DEPTH + 1) // 3
N_NA_LAYERS = DEPTH // 3
N_MOD = 6
D_FF = 4 * D_MODEL
NORM_EPS = 1e-6
ROPE_BASE = 10000.0

RET_HEADS = 8
RET_DK = D_MODEL // RET_HEADS
RET_DV = 2 * D_MODEL // RET_HEADS
RET_QK = RET_HEADS * RET_DK
RET_V = RET_HEADS * RET_DV
RET_CHUNK = 128

GDN_DK = 128
GDN_DV = 128
GDN_QK_HEADS = D_MODEL // 128
GDN_V_HEADS = D_MODEL // 64
GDN_QK_DIM = GDN_QK_HEADS * GDN_DK
GDN_V_DIM = GDN_V_HEADS * GDN_DV
GDN_CONV_DIM = 2 * GDN_QK_DIM + GDN_V_DIM
GDN_CONV = 5
GDN_CHUNK = 64

NA_HEADS = 16
NA_HD = D_MODEL // NA_HEADS
WIN_H = 8
WIN_W = 16
NA_QROWS = 2

kernel_name = 'hybrid_ret_gdn_natten_dit_trunk'


def rms_norm(x, g):
    xf = x.astype(jnp.float32)
    y = xf * lax.rsqrt(jnp.mean(xf * xf, axis=-1, keepdims=True) + NORM_EPS)
    return (y * g.astype(jnp.float32)).astype(x.dtype)


def head_rms(y):
    return y * lax.rsqrt(jnp.mean(y * y, axis=-1, keepdims=True) + NORM_EPS)


def l2norm(t):
    t = t.astype(jnp.float32)
    return t * lax.rsqrt(jnp.sum(t * t, axis=-1, keepdims=True) + 1e-6)


def to_heads(t, n):
    b, s, _ = t.shape
    return t.reshape(b, s, n, -1).transpose(0, 2, 1, 3)


def merge_heads(t):
    b, h, s, d = t.shape
    return t.transpose(0, 2, 1, 3).reshape(b, s, h * d)


def same_time(t):
    return t


def flip_time(t):
    return jnp.flip(t, axis=2)


def axial_rope(n_tokens, head_dim):
    t = jnp.arange(n_tokens)
    row = (t // GRID_W).astype(jnp.float32)
    col = (t % GRID_W).astype(jnp.float32)
    n_pairs = head_dim // 2
    inv = ROPE_BASE ** (-jnp.arange(0, n_pairs, 2, dtype=jnp.float32) / n_pairs)
    ang = jnp.concatenate([row[:, None] * inv, col[:, None] * inv], axis=-1)
    return jnp.cos(ang), jnp.sin(ang)


def apply_rope(x, cos, sin):
    x1, x2 = jnp.split(x, 2, axis=-1)
    c, s = cos.astype(x.dtype), sin.astype(x.dtype)
    return jnp.concatenate([x1 * c - x2 * s, x1 * s + x2 * c], axis=-1)


def short_conv(x, w):
    k = w.shape[0]
    return lax.conv_general_dilated(
        x, w[:, None, :].astype(x.dtype), window_strides=(1,),
        padding=[((k - 1) // 2, k // 2)], dimension_numbers=('NWC', 'WIO', 'NWC'),
        feature_group_count=x.shape[-1])


def sq_relu_mlp(h, w1, w2):
    return jnp.square(jax.nn.relu(h @ w1)) @ w2


def retention_log_decays():
    fwd = jnp.log(1.0 - 2.0 ** (-5.0 - jnp.arange(RET_HEADS, dtype=jnp.float32)))
    return jnp.stack([fwd, fwd[::-1]])


def retention_scan(q, k, v, log_gamma, state0, with_out):
    b, h, t, _ = k.shape
    dv = v.shape[-1]
    c = RET_CHUNK
    n = t // c
    pos = jnp.arange(c, dtype=jnp.float32)
    lg = log_gamma[:, None]
    zeta = jnp.exp(lg * (c - 1 - pos))
    xi = jnp.exp(lg * (pos + 1))
    chunk_decay = jnp.exp(log_gamma * c)

    def chunks(a):
        return jnp.moveaxis(a.astype(jnp.float32).reshape(b, h, n, c, a.shape[-1]), 2, 0)

    def update(R, kb, vb):
        return R * chunk_decay[None, :, None, None] + jnp.einsum(
            'bhcd,bhce->bhde', kb, vb * zeta[None, :, :, None])

    if not with_out:
        def step_state(R, inp):
            return update(R, inp[0], inp[1]), None
        R, _ = lax.scan(step_state, state0, (chunks(k), chunks(v)))
        return R, None

    dist = pos[:, None] - pos[None, :]
    intra = jnp.where(dist >= 0, jnp.exp(lg[:, :, None] * jnp.maximum(dist, 0.0)), 0.0)

    def step(R, inp):
        qb, kb, vb = inp
        s = jnp.einsum('bhcd,bhmd->bhcm', qb, kb) * intra[None]
        o = jnp.einsum('bhcm,bhme->bhce', s, vb) + jnp.einsum(
            'bhcd,bhde->bhce', qb, R) * xi[None, :, :, None]
        return update(R, kb, vb), o

    R, o = lax.scan(step, state0, (chunks(q), chunks(k), chunks(v)))
    return R, jnp.moveaxis(o, 0, 2).reshape(b, h, t, dv)


def retention_output(y, g, w_out):
    yn = merge_heads(head_rms(y))
    return (jax.nn.silu(g.astype(jnp.float32)) * yn).astype(g.dtype) @ w_out


def retention_mixer(h_lat, h_ctx, w_in, w_out, with_ctx_out):
    b, t, _ = h_lat.shape

    def project(h):
        q, k, v, g = jnp.split(h @ w_in, [RET_QK, 2 * RET_QK, 2 * RET_QK + RET_V], axis=-1)
        return (to_heads(q, RET_HEADS), to_heads(k, RET_HEADS) * RET_DK ** -0.5,
                to_heads(v, RET_HEADS), g)

    cos, sin = axial_rope(t, RET_DK)
    ql, kl, vl, gl = project(h_lat)
    ql, kl = apply_rope(ql, cos, sin), apply_rope(kl, cos, sin)
    qc, kc, vc, gc = project(h_ctx)
    log_gammas = retention_log_decays()
    outs_lat, outs_ctx = [], []
    for d, f in enumerate((same_time, flip_time)):
        zero = jnp.zeros((b, RET_HEADS, RET_DK, RET_DV), jnp.float32)
        s_ctx, o_ctx = retention_scan(f(qc), f(kc), f(vc), log_gammas[d], zero, with_ctx_out)
        _, o_lat = retention_scan(f(ql), f(kl), f(vl), log_gammas[d], s_ctx, True)
        outs_lat.append(f(o_lat))
        if with_ctx_out:
            outs_ctx.append(f(o_ctx))
    y_lat = retention_output(outs_lat[0] + outs_lat[1], gl, w_out)
    y_ctx = retention_output(outs_ctx[0] + outs_ctx[1], gc, w_out) if with_ctx_out else None
    return y_lat, y_ctx


def gated_delta_scan(q, k, v, g, beta, state0, with_out):
    b, h, t, _ = k.shape
    dv = v.shape[-1]
    c = GDN_CHUNK
    n = t // c

    def chunks(a):
        return jnp.moveaxis(a.reshape(b, h, n, c, *a.shape[3:]), 2, 0)

    tril = jnp.tril(jnp.ones((c, c), dtype=bool))
    strict = jnp.tril(jnp.ones((c, c), dtype=bool), -1)
    eye = jnp.eye(c, dtype=jnp.float32)

    def step(S, inp):
        kb, vb, gb, bb = inp[0], inp[1], inp[2], inp[3]
        gb = jnp.cumsum(gb, axis=-1)
        decay = jnp.exp(jnp.where(tril, gb[..., :, None] - gb[..., None, :], -jnp.inf))
        m = jnp.where(strict, jnp.einsum('bhcd,bhmd->bhcm', kb, kb) * decay, 0.0) * bb[..., :, None]
        rhs = jnp.concatenate([vb * bb[..., None], kb * (bb * jnp.exp(gb))[..., None]], axis=-1)
        sol = lax.linalg.triangular_solve(m + eye, rhs, left_side=True, lower=True)
        u, w = sol[..., :dv], sol[..., dv:]
        v_new = u - jnp.einsum('bhcd,bhde->bhce', w, S)
        g_last = gb[..., -1:]
        S_new = S * jnp.exp(g_last)[..., None] + jnp.einsum(
            'bhcd,bhce->bhde', kb * jnp.exp(g_last - gb)[..., None], v_new)
        if not with_out:
            return S_new, None
        qb = inp[4]
        a = jnp.einsum('bhcd,bhmd->bhcm', qb, kb) * decay
        o = jnp.einsum('bhcd,bhde->bhce', qb * jnp.exp(gb)[..., None], S) + jnp.einsum(
            'bhcm,bhme->bhce', a, v_new)
        return S_new, o

    xs = (chunks(k), chunks(v), chunks(g), chunks(beta))
    if with_out:
        xs = (xs[0], xs[1], xs[2], xs[3], chunks(q))
    S, o = lax.scan(step, state0, xs)
    if not with_out:
        return S, None
    return S, jnp.moveaxis(o, 0, 2).reshape(b, h, t, dv)


def gdn_project(h, w_in, conv_w):
    z = h @ w_in
    qkv = jax.nn.silu(short_conv(z[..., :GDN_CONV_DIM], conv_w))
    gate = z[..., GDN_CONV_DIM:]
    q, k, v = jnp.split(qkv, [GDN_QK_DIM, 2 * GDN_QK_DIM], axis=-1)
    rep = GDN_V_HEADS // GDN_QK_HEADS
    q = jnp.repeat(l2norm(to_heads(q, GDN_QK_HEADS)), rep, axis=1) * GDN_DK ** -0.5
    k = jnp.repeat(l2norm(to_heads(k, GDN_QK_HEADS)), rep, axis=1)
    v = to_heads(v, GDN_V_HEADS).astype(jnp.float32)
    return q, k, v, gate


def gdn_gates(h, w_ab, a_log, dt_bias):
    ab = (h @ w_ab).astype(jnp.float32)
    a, bt = ab[..., :GDN_V_HEADS], ab[..., GDN_V_HEADS:]
    g = -jnp.exp(a_log.astype(jnp.float32)) * jax.nn.softplus(a + dt_bias.astype(jnp.float32))
    return g.transpose(0, 2, 1), jax.nn.sigmoid(bt).transpose(0, 2, 1)


def gdn_output(y, gate, norm_w, w_out):
    yn = merge_heads(head_rms(y) * norm_w.astype(jnp.float32))
    return (yn * jax.nn.silu(gate.astype(jnp.float32))).astype(gate.dtype) @ w_out


def gdn_mixer(h_lat, h_ctx, w_in, conv_w, w_ab, a_log, dt_bias, norm_w, w_out, with_ctx_out):
    b = h_lat.shape[0]
    ql, kl, vl, zl = gdn_project(h_lat, w_in, conv_w)
    qc, kc, vc, zc = gdn_project(h_ctx, w_in, conv_w)
    outs_lat, outs_ctx = [], []
    for d, f in enumerate((same_time, flip_time)):
        gl, bl = gdn_gates(h_lat, w_ab[d], a_log[d], dt_bias[d])
        gc, bc = gdn_gates(h_ctx, w_ab[d], a_log[d], dt_bias[d])
        zero = jnp.zeros((b, GDN_V_HEADS, GDN_DK, GDN_DV), jnp.float32)
        s_ctx, o_ctx = gated_delta_scan(f(qc), f(kc), f(vc), f(gc), f(bc), zero, with_ctx_out)
        _, o_lat = gated_delta_scan(f(ql), f(kl), f(vl), f(gl), f(bl), s_ctx, True)
        outs_lat.append(f(o_lat))
        if with_ctx_out:
            outs_ctx.append(f(o_ctx))
    y_lat = gdn_output(outs_lat[0] + outs_lat[1], zl, norm_w, w_out)
    y_ctx = gdn_output(outs_ctx[0] + outs_ctx[1], zc, norm_w, w_out) if with_ctx_out else None
    return y_lat, y_ctx


def na_mixer(h_lat, h_ctx, w_in, w_out, rpb, with_ctx_out):
    b, t, _ = h_lat.shape
    rows = t // GRID_W
    kh, kw = min(WIN_H, rows), min(WIN_W, GRID_W)
    nb = min(kh + NA_QROWS - 1, rows)
    qblk = NA_QROWS * GRID_W
    nloc = nb * GRID_W
    scale = NA_HD ** -0.5

    def project(h):
        q, k, v = jnp.split(h @ w_in, 3, axis=-1)
        return to_heads(q, NA_HEADS) * scale, to_heads(k, NA_HEADS), to_heads(v, NA_HEADS)

    q, k, v = project(h_lat)
    qc, kc, vc = project(h_ctx)
    k_grid = k.reshape(b, NA_HEADS, rows, GRID_W, NA_HD)
    v_grid = v.reshape(b, NA_HEADS, rows, GRID_W, NA_HD)
    q_blocks = jnp.moveaxis(q.reshape(b, NA_HEADS, rows // NA_QROWS, qblk, NA_HD), 2, 0)
    row0 = jnp.arange(rows // NA_QROWS, dtype=jnp.int32) * NA_QROWS
    lq = jnp.arange(qblk, dtype=jnp.int32)
    lk = jnp.arange(nloc, dtype=jnp.int32)
    q_col, k_col = lq % GRID_W, lk % GRID_W
    col_start = jnp.clip(q_col - kw // 2, 0, GRID_W - kw)
    col_ok = (k_col[None] >= col_start[:, None]) & (k_col[None] < col_start[:, None] + kw)
    dc = jnp.clip(k_col[None] - q_col[:, None] + WIN_W - 1, 0, 2 * WIN_W - 2)

    def attend(inp):
        qb, r0 = inp
        b0 = jnp.clip(r0 - kh // 2, 0, rows - nb)
        kb = lax.dynamic_slice_in_dim(k_grid, b0, nb, axis=2).reshape(b, NA_HEADS, nloc, NA_HD)
        vb = lax.dynamic_slice_in_dim(v_grid, b0, nb, axis=2).reshape(b, NA_HEADS, nloc, NA_HD)
        q_row = r0 + lq // GRID_W
        k_row = b0 + lk // GRID_W
        row_start = jnp.clip(q_row - kh // 2, 0, rows - kh)
        row_ok = (k_row[None] >= row_start[:, None]) & (k_row[None] < row_start[:, None] + kh)
        dr = jnp.clip(k_row[None] - q_row[:, None] + WIN_H - 1, 0, 2 * WIN_H - 2)
        s_loc = jnp.einsum('bhqd,bhkd->bhqk', qb, kb, preferred_element_type=jnp.float32)
        s_loc = jnp.where(row_ok & col_ok, s_loc + rpb[:, dr, dc].astype(jnp.float32), -jnp.inf)
        s_ctx = jnp.einsum('bhqd,bhkd->bhqk', qb, kc, preferred_element_type=jnp.float32)
        p = jax.nn.softmax(jnp.concatenate([s_loc, s_ctx], axis=-1), axis=-1).astype(vb.dtype)
        return (jnp.einsum('bhqk,bhkd->bhqd', p[..., :nloc], vb)
                + jnp.einsum('bhqk,bhkd->bhqd', p[..., nloc:], vc))

    o = lax.map(attend, (q_blocks, row0))
    o = jnp.moveaxis(o, 0, 2).reshape(b, NA_HEADS, t, NA_HD)
    y_lat = merge_heads(o) @ w_out
    y_ctx = None
    if with_ctx_out:
        pc = jax.nn.softmax(jnp.einsum('bhqd,bhkd->bhqk', qc, kc, preferred_element_type=jnp.float32),
                            axis=-1).astype(vc.dtype)
        y_ctx = merge_heads(jnp.einsum('bhqk,bhkd->bhqd', pc, vc)) @ w_out
    return y_lat, y_ctx


def setup_inputs(seed: int = 0) -> dict:
    key = jax.random.key(seed)
    keys = iter(jax.random.split(key, 32))

    def normal(shape, std):
        return jax.random.normal(next(keys), shape, jnp.float32) * std

    D = D_MODEL
    x = normal((BATCH, SEQ, D), 1.0)
    c = normal((BATCH, D), 1.0)
    ctx = normal((BATCH, CTX_LEN, D), 1.0)
    c_ctx = normal((D,), 1.0)
    ada_w = normal((DEPTH, D, N_MOD * D), 0.5 * D ** -0.5)
    ada_b = normal((DEPTH, N_MOD * D), 0.02)
    norm_g = 1.0 + normal((DEPTH, 4, D), 0.05)
    mlp_w1 = normal((DEPTH, D, D_FF), D ** -0.5)
    mlp_w2 = normal((DEPTH, D_FF, D), D_FF ** -0.5)
    ret_w_in = normal((N_RET_LAYERS, D, 2 * RET_QK + 2 * RET_V), D ** -0.5)
    ret_w_out = normal((N_RET_LAYERS, RET_V, D), RET_V ** -0.5)
    gdn_w_in = normal((N_GDN_LAYERS, D, GDN_CONV_DIM + GDN_V_DIM), D ** -0.5)
    gdn_conv_w = normal((N_GDN_LAYERS, GDN_CONV, GDN_CONV_DIM), GDN_CONV ** -0.5)
    gdn_w_ab = normal((N_GDN_LAYERS, 2, D, 2 * GDN_V_HEADS), D ** -0.5)
    gdn_a_log = jnp.log(jax.random.uniform(next(keys), (N_GDN_LAYERS, 2, GDN_V_HEADS),
                                           jnp.float32, 1.0, 16.0))
    dt = jnp.exp(jax.random.uniform(next(keys), (N_GDN_LAYERS, 2, GDN_V_HEADS), jnp.float32,
                                    math.log(1e-3), math.log(1e-1)))
    gdn_dt_bias = dt + jnp.log(-jnp.expm1(-dt))
    gdn_norm_w = 1.0 + normal((N_GDN_LAYERS, GDN_DV), 0.05)
    gdn_w_out = normal((N_GDN_LAYERS, GDN_V_DIM, D), GDN_V_DIM ** -0.5)
    na_w_in = normal((N_NA_LAYERS, D, 3 * D), D ** -0.5)
    na_w_out = normal((N_NA_LAYERS, D, D), D ** -0.5)
    na_rpb = normal((N_NA_LAYERS, NA_HEADS, 2 * WIN_H - 1, 2 * WIN_W - 1), 0.1)
    return {'x': x, 'c': c, 'ctx': ctx, 'c_ctx': c_ctx, 'ada_w': ada_w, 'ada_b': ada_b,
            'norm_g': norm_g, 'mlp_w1': mlp_w1, 'mlp_w2': mlp_w2,
            'ret_w_in': ret_w_in, 'ret_w_out': ret_w_out,
            'gdn_w_in': gdn_w_in, 'gdn_conv_w': gdn_conv_w, 'gdn_w_ab': gdn_w_ab,
            'gdn_a_log': gdn_a_log, 'gdn_dt_bias': gdn_dt_bias, 'gdn_norm_w': gdn_norm_w,
            'gdn_w_out': gdn_w_out, 'na_w_in': na_w_in, 'na_w_out': na_w_out, 'na_rpb': na_rpb}


def reference(x, c, ctx, c_ctx, ada_w, ada_b, norm_g, mlp_w1, mlp_w2, ret_w_in, ret_w_out,
              gdn_w_in, gdn_conv_w, gdn_w_ab, gdn_a_log, gdn_dt_bias, gdn_norm_w, gdn_w_out,
              na_w_in, na_w_out, na_rpb):
    D = D_MODEL
    s_lat = jax.nn.silu(c)
    s_ctx = jax.nn.silu(c_ctx)
    i_ret = i_gdn = i_na = 0
    for i in range(DEPTH):
        last = i == DEPTH - 1
        m_lat = jnp.split((s_lat @ ada_w[i] + ada_b[i])[:, None, :], N_MOD, axis=-1)
        n_ctx_mod = 2 if last else N_MOD
        m_ctx = jnp.split(s_ctx @ ada_w[i][:, :n_ctx_mod * D] + ada_b[i][:n_ctx_mod * D],
                          n_ctx_mod, axis=-1)
        h_lat = rms_norm(x, norm_g[i, 0]) * (1 + m_lat[1]) + m_lat[0]
        h_ctx = rms_norm(ctx, norm_g[i, 0]) * (1 + m_ctx[1]) + m_ctx[0]
        kind = i % N_MIXERS
        if kind == 0:
            y_lat, y_ctx = retention_mixer(h_lat, h_ctx, ret_w_in[i_ret], ret_w_out[i_ret], not last)
            i_ret += 1
        elif kind == 1:
            y_lat, y_ctx = gdn_mixer(h_lat, h_ctx, gdn_w_in[i_gdn], gdn_conv_w[i_gdn], gdn_w_ab[i_gdn],
                                     gdn_a_log[i_gdn], gdn_dt_bias[i_gdn], gdn_norm_w[i_gdn],
                                     gdn_w_out[i_gdn], not last)
            i_gdn += 1
        else:
            y_lat, y_ctx = na_mixer(h_lat, h_ctx, na_w_in[i_na], na_w_out[i_na], na_rpb[i_na], not last)
            i_na += 1
        x = x + m_lat[2] * rms_norm(y_lat, norm_g[i, 1])
        h = rms_norm(x, norm_g[i, 2]) * (1 + m_lat[4]) + m_lat[3]
        x = x + m_lat[5] * rms_norm(sq_relu_mlp(h, mlp_w1[i], mlp_w2[i]), norm_g[i, 3])
        if not last:
            ctx = ctx + m_ctx[2] * rms_norm(y_ctx, norm_g[i, 1])
            hc = rms_norm(ctx, norm_g[i, 2]) * (1 + m_ctx[4]) + m_ctx[3]
            ctx = ctx + m_ctx[5] * rms_norm(sq_relu_mlp(hc, mlp_w1[i], mlp_w2[i]), norm_g[i, 3])
    return x
```

```python
import functools
import math

import numpy as np
import jax
import jax.numpy as jnp
from jax import lax
from jax.experimental import pallas as pl
from jax.experimental.pallas import tpu as pltpu

F32 = jnp.float32
BF16 = jnp.bfloat16

NORM_EPS = 1e-6
L2_EPS = 1e-6
ROPE_BASE = 10000.0
N_MOD = 6
GRID_W = 64

RET_HEADS = 8
RET_CHUNK = 128
GDN_HEAD_DIM = 128
GDN_CHUNK = 128
GDN_CONV = 5
NA_HEADS = 16
WIN_H = 8
WIN_W = 16
NA_QROWS = 2
NA_KROWS = 10
NEG_BIG = -1e30

V7X_VMEM_LIMIT_BYTES = 56 * 1024 * 1024


def _cparams(sem):
    return pltpu.CompilerParams(dimension_semantics=sem, vmem_limit_bytes=V7X_VMEM_LIMIT_BYTES)


def _sigmoid(x):
    return 1.0 / (1.0 + jnp.exp(-x))


def _silu(x):
    return x * _sigmoid(x)


def _dot(a, b):
    return jnp.dot(a, b, preferred_element_type=F32)


def _dot_nt(a, b):
    return lax.dot_general(a, b, (((1,), (1,)), ((), ())), preferred_element_type=F32)


def _dot_tn(a, b):
    return lax.dot_general(a, b, (((0,), (0,)), ((), ())), preferred_element_type=F32)


def _rms(y, g):
    return y * lax.rsqrt(jnp.mean(y * y, axis=-1, keepdims=True) + NORM_EPS) * g


def _ada_kernel(c_ref, w_ref, b_ref, o_ref):
    s = _silu(c_ref[...]).astype(BF16)
    o_ref[0] = _dot(s, w_ref[0].astype(BF16)) + b_ref[0]


def ada_modulation(cvec, ada_w, ada_b, tn=1024):
    depth, d, n = ada_w.shape
    r = cvec.shape[0]
    return pl.pallas_call(
        _ada_kernel,
        grid=(depth, n // tn),
        in_specs=[pl.BlockSpec((r, d), lambda l, j: (0, 0)),
                  pl.BlockSpec((1, d, tn), lambda l, j: (l, 0, j)),
                  pl.BlockSpec((1, 1, tn), lambda l, j: (l, 0, j))],
        out_specs=pl.BlockSpec((1, r, tn), lambda l, j: (l, 0, j)),
        out_shape=jax.ShapeDtypeStruct((depth, r, n), F32),
        compiler_params=_cparams(("parallel", "parallel")),
        name="ada_modulation",
    )(cvec, ada_w, ada_b.reshape(depth, 1, n))


def _modulated_norm(x, g, m_ref, shift_idx, scale_idx):
    y = _rms(x, g)
    return y * (1.0 + m_ref[0, scale_idx:scale_idx + 1, :]) + m_ref[0, shift_idx:shift_idx + 1, :]


def _nmm_kernel(x_ref, m_ref, g_ref, w_ref, o_ref, h_ref, *, shift_idx, scale_idx):
    @pl.when(pl.program_id(1) == 0)
    def _():
        h_ref[...] = _modulated_norm(x_ref[...], g_ref[...], m_ref, shift_idx, scale_idx).astype(BF16)

    o_ref[...] = _dot(h_ref[...], w_ref[...]).astype(o_ref.dtype)


def norm_mod_matmul(x, mods, tokens_per_mod, g, w, shift_idx, scale_idx, out_dtype, tm, tn):
    n, d = x.shape
    nout = w.shape[1]
    tm = min(tm, tokens_per_mod)
    tn = min(tn, nout)
    bpm = tokens_per_mod // tm
    return pl.pallas_call(
        functools.partial(_nmm_kernel, shift_idx=shift_idx, scale_idx=scale_idx),
        grid=(n // tm, nout // tn),
        in_specs=[pl.BlockSpec((tm, d), lambda i, j: (i, 0)),
                  pl.BlockSpec((1, N_MOD, d), lambda i, j: (i // bpm, 0, 0)),
                  pl.BlockSpec((1, d), lambda i, j: (0, 0)),
                  pl.BlockSpec((d, tn), lambda i, j: (0, j))],
        out_specs=pl.BlockSpec((tm, tn), lambda i, j: (i, j)),
        out_shape=jax.ShapeDtypeStruct((n, nout), out_dtype),
        scratch_shapes=[pltpu.VMEM((tm, d), BF16)],
        compiler_params=_cparams(("parallel", "arbitrary")),
        name="norm_mod_matmul",
    )(x, mods, g.reshape(1, d), w)


def _out_res_kernel(a_ref, w_ref, x_ref, m_ref, g_ref, o_ref, acc_ref, *, gate_idx, nk):
    k = pl.program_id(1)

    @pl.when(k == 0)
    def _():
        acc_ref[...] = jnp.zeros_like(acc_ref)

    acc_ref[...] += _dot(a_ref[...], w_ref[...])

    @pl.when(k == nk - 1)
    def _():
        yn = _rms(acc_ref[...], g_ref[...])
        o_ref[...] = x_ref[...] + m_ref[0, gate_idx:gate_idx + 1, :] * yn


def out_proj_residual(a, w, x, mods, tokens_per_mod, g, gate_idx, tm, tk):
    n, kdim = a.shape
    d = w.shape[1]
    tm = min(tm, tokens_per_mod)
    tk = min(tk, kdim)
    bpm = tokens_per_mod // tm
    nk = kdim // tk
    return pl.pallas_call(
        functools.partial(_out_res_kernel, gate_idx=gate_idx, nk=nk),
        grid=(n // tm, nk),
        in_specs=[pl.BlockSpec((tm, tk), lambda i, k: (i, k)),
                  pl.BlockSpec((tk, d), lambda i, k: (k, 0)),
                  pl.BlockSpec((tm, d), lambda i, k: (i, 0)),
                  pl.BlockSpec((1, N_MOD, d), lambda i, k: (i // bpm, 0, 0)),
                  pl.BlockSpec((1, d), lambda i, k: (0, 0))],
        out_specs=pl.BlockSpec((tm, d), lambda i, k: (i, 0)),
        out_shape=jax.ShapeDtypeStruct((n, d), F32),
        scratch_shapes=[pltpu.VMEM((tm, d), F32)],
        compiler_params=_cparams(("parallel", "arbitrary")),
        name="out_proj_residual",
    )(a, w, x, mods, g.reshape(1, d))


def _mlp_kernel(x_ref, m_ref, g2_ref, g3_ref, w1_ref, w2_ref, o_ref, h_ref, acc_ref, *, nk):
    k = pl.program_id(1)

    @pl.when(k == 0)
    def _():
        h_ref[...] = _modulated_norm(x_ref[...], g2_ref[...], m_ref, 3, 4).astype(BF16)
        acc_ref[...] = jnp.zeros_like(acc_ref)

    a = jnp.maximum(_dot(h_ref[...], w1_ref[...]), 0.0)
    acc_ref[...] += _dot((a * a).astype(BF16), w2_ref[...])

    @pl.when(k == nk - 1)
    def _():
        yn = _rms(acc_ref[...], g3_ref[...])
        o_ref[...] = x_ref[...] + m_ref[0, 5:6, :] * yn


def mlp_residual(x, mods, tokens_per_mod, g2, g3, w1, w2, tm, tf):
    n, d = x.shape
    dff = w1.shape[1]
    tm = min(tm, tokens_per_mod)
    tf = min(tf, dff)
    bpm = tokens_per_mod // tm
    nk = dff // tf
    return pl.pallas_call(
        functools.partial(_mlp_kernel, nk=nk),
        grid=(n // tm, nk),
        in_specs=[pl.BlockSpec((tm, d), lambda i, k: (i, 0)),
                  pl.BlockSpec((1, N_MOD, d), lambda i, k: (i // bpm, 0, 0)),
                  pl.BlockSpec((1, d), lambda i, k: (0, 0)),
                  pl.BlockSpec((1, d), lambda i, k: (0, 0)),
                  pl.BlockSpec((d, tf), lambda i, k: (0, k)),
                  pl.BlockSpec((tf, d), lambda i, k: (k, 0))],
        out_specs=pl.BlockSpec((tm, d), lambda i, k: (i, 0)),
        out_shape=jax.ShapeDtypeStruct((n, d), F32),
        scratch_shapes=[pltpu.VMEM((tm, d), BF16), pltpu.VMEM((tm, d), F32)],
        compiler_params=_cparams(("parallel", "arbitrary")),
        name="mlp_residual",
    )(x, mods, g2.reshape(1, d), g3.reshape(1, d), w1, w2)


def _ret_kernel(lg_ref, cos_ref, sin_ref, qc_ref, kc_ref, vc_ref, gc_ref, ql_ref, kl_ref, vl_ref, gl_ref,
                *rest, tc, tl, dk, dv, with_ctx_out):
    if with_ctx_out:
        ac_ref, al_ref, qs_ref, ks_ref, of_ref, ob_ref, rf_ref, rb_ref = rest
    else:
        al_ref, qs_ref, ks_ref, of_ref, ob_ref, rf_ref, rb_ref = rest
        ac_ref = None
    c = RET_CHUNK
    half = dk // 2
    k_scale = dk ** -0.5

    qs_ref[0:tc, :] = qc_ref[...]
    ks_ref[0:tc, :] = (kc_ref[...].astype(F32) * k_scale).astype(BF16)

    rt = 256

    def rope_body(i, carry):
        r0 = pl.multiple_of(i * rt, rt)
        cs = cos_ref[pl.ds(r0, rt), :]
        sn = sin_ref[pl.ds(r0, rt), :]
        for src, dst, scale in ((ql_ref, qs_ref, 1.0), (kl_ref, ks_ref, k_scale)):
            t = src[pl.ds(r0, rt), :].astype(F32) * scale
            t1, t2 = t[:, :half], t[:, half:]
            dst[pl.ds(tc + r0, rt), :] = jnp.concatenate(
                [t1 * cs - t2 * sn, t1 * sn + t2 * cs], axis=-1).astype(BF16)
        return carry

    lax.fori_loop(0, tl // rt, rope_body, 0)

    lgf = lg_ref[0, 0:1, 0:1]
    lgb = lg_ref[0, 1:2, 0:1]
    ci = lax.broadcasted_iota(jnp.int32, (c, c), 0)
    mi = lax.broadcasted_iota(jnp.int32, (c, c), 1)
    dist = (ci - mi).astype(F32)
    intra_f = jnp.where(dist >= 0, jnp.exp(lgf * jnp.maximum(dist, 0.0)), 0.0)
    intra_b = jnp.where(dist <= 0, jnp.exp(lgb * jnp.maximum(-dist, 0.0)), 0.0)
    pos = lax.broadcasted_iota(jnp.int32, (c, 1), 0).astype(F32)
    xi_f = jnp.exp(lgf * (pos + 1.0))
    zeta_f = jnp.exp(lgf * (c - 1.0 - pos))
    xi_b = jnp.exp(lgb * (c - pos))
    zeta_b = jnp.exp(lgb * pos)
    cd_f = jnp.exp(lgf * float(c))
    cd_b = jnp.exp(lgb * float(c))

    def step(row, vt, r_ref, intra, xi, zeta, cd):
        qt = qs_ref[pl.ds(row, c), :]
        kt = ks_ref[pl.ds(row, c), :]
        s = _dot_nt(qt, kt) * intra
        r = r_ref[...]
        o = _dot(s.astype(BF16), vt) + _dot(qt, r.astype(BF16)) * xi
        vz = (vt.astype(F32) * zeta).astype(BF16)
        r_ref[...] = r * cd + _dot_tn(kt, vz)
        return o

    rf_ref[...] = jnp.zeros_like(rf_ref)
    rb_ref[...] = jnp.zeros_like(rb_ref)

    def make_body(v_ref, base, n):
        def body(i, carry):
            lf = pl.multiple_of(i * c, c)
            lb = pl.multiple_of((n - 1 - i) * c, c)
            of_ref[pl.ds(base + lf, c), :] = step(base + lf, v_ref[pl.ds(lf, c), :], rf_ref,
                                                  intra_f, xi_f, zeta_f, cd_f)
            ob_ref[pl.ds(base + lb, c), :] = step(base + lb, v_ref[pl.ds(lb, c), :], rb_ref,
                                                  intra_b, xi_b, zeta_b, cd_b)
            return carry
        return body

    lax.fori_loop(0, tc // c, make_body(vc_ref, 0, tc // c), 0)
    lax.fori_loop(0, tl // c, make_body(vl_ref, tc, tl // c), 0)

    def finish(g_ref, a_ref, base, n):
        def body(i, carry):
            r0 = pl.multiple_of(i * rt, rt)
            y = of_ref[pl.ds(base + r0, rt), :] + ob_ref[pl.ds(base + r0, rt), :]
            yn = y * lax.rsqrt(jnp.mean(y * y, axis=-1, keepdims=True) + NORM_EPS)
            g = g_ref[pl.ds(r0, rt), :].astype(F32)
            a_ref[pl.ds(r0, rt), :] = (_silu(g) * yn).astype(BF16)
            return carry
        lax.fori_loop(0, n // rt, body, 0)

    if with_ctx_out:
        finish(gc_ref, ac_ref, 0, tc)
    finish(gl_ref, al_ref, tc, tl)


def retention_mixer(zc, zl, batch, lg, cos, sin, with_ctx_out):
    tc = zc.shape[0] // batch
    tl = zl.shape[0] // batch
    width = zc.shape[1]
    dk = width // 6 // RET_HEADS
    dv = 2 * dk
    h = RET_HEADS
    kern = functools.partial(_ret_kernel, tc=tc, tl=tl, dk=dk, dv=dv, with_ctx_out=with_ctx_out)

    def seg_specs(t):
        return [pl.BlockSpec((t, dk), lambda b, j: (b, j)),
                pl.BlockSpec((t, dk), lambda b, j: (b, h + j)),
                pl.BlockSpec((t, dv), lambda b, j: (b, h + j)),
                pl.BlockSpec((t, dv), lambda b, j: (b, 2 * h + j))]

    out_shape = [jax.ShapeDtypeStruct((batch * tl, h * dv), BF16)]
    out_specs = [pl.BlockSpec((tl, dv), lambda b, j: (b, j))]
    if with_ctx_out:
        out_shape = [jax.ShapeDtypeStruct((batch * tc, h * dv), BF16)] + out_shape
        out_specs = [pl.BlockSpec((tc, dv), lambda b, j: (b, j))] + out_specs
    outs = pl.pallas_call(
        kern,
        grid=(batch, h),
        in_specs=[pl.BlockSpec((1, 8, 128), lambda b, j: (j, 0, 0)),
                  pl.BlockSpec((tl, dk // 2), lambda b, j: (0, 0)),
                  pl.BlockSpec((tl, dk // 2), lambda b, j: (0, 0))] + seg_specs(tc) + seg_specs(tl),
        out_specs=out_specs,
        out_shape=out_shape,
        scratch_shapes=[pltpu.VMEM((tc + tl, dk), BF16), pltpu.VMEM((tc + tl, dk), BF16),
                        pltpu.VMEM((tc + tl, dv), F32), pltpu.VMEM((tc + tl, dv), F32),
                        pltpu.VMEM((dk, dv), F32), pltpu.VMEM((dk, dv), F32)],
        compiler_params=_cparams(("parallel", "parallel")),
        name="retention_mixer",
    )(lg, cos, sin, zc, zc, zc, zc, zl, zl, zl, zl)
    if with_ctx_out:
        return outs[1], outs[0]
    return outs[0], None


def retention_tables(t, dk):
    fwd = jnp.log(1.0 - 2.0 ** (-5.0 - jnp.arange(RET_HEADS, dtype=F32)))
    lg = jnp.zeros((RET_HEADS, 8, 128), F32)
    lg = lg.at[:, 0, :].set(fwd[:, None]).at[:, 1, :].set(fwd[::-1][:, None])
    tt = jnp.arange(t)
    row = (tt // GRID_W).astype(F32)
    col = (tt % GRID_W).astype(F32)
    n_pairs = dk // 2
    inv = ROPE_BASE ** (-jnp.arange(0, n_pairs, 2, dtype=F32) / n_pairs)
    ang = jnp.concatenate([row[:, None] * inv, col[:, None] * inv], axis=-1)
    return lg, jnp.cos(ang), jnp.sin(ang)


def _split3(x):
    hi = x.astype(BF16)
    r1 = x - hi.astype(F32)
    mid = r1.astype(BF16)
    lo = (r1 - mid.astype(F32)).astype(BF16)
    return hi, mid, lo


def _gdn_gates_kernel(ab_ref, alog_ref, dtb_ref, o_ref, *, t):
    c = GDN_CHUNK
    nrow = ab_ref.shape[1]
    rowid = lax.broadcasted_iota(jnp.int32, (nrow, 1), 0) % 8
    mi = lax.broadcasted_iota(jnp.int32, (c, c), 0)
    ci = lax.broadcasted_iota(jnp.int32, (c, c), 1)
    pre = (mi <= ci).astype(BF16)
    suf = (mi >= ci).astype(BF16)
    a_neg = -jnp.exp(alog_ref[...])
    dtb = dtb_ref[...]
    for w in range(t // c):
        x = ab_ref[0, :, w * c:(w + 1) * c]
        z = x + dtb
        g = a_neg * (jnp.maximum(z, 0.0) + jnp.log1p(jnp.exp(-jnp.abs(z))))
        parts = _split3(g)
        cp = sum(_dot(p, pre) for p in parts)
        cs = sum(_dot(p, suf) for p in parts)
        o_ref[0, :, w * c:(w + 1) * c] = jnp.where(rowid < 2, cp, jnp.where(rowid < 4, cs, _sigmoid(x)))


def gdn_gates(ab_t, alog_col, dtb_col):
    b, nrow, t = ab_t.shape
    return pl.pallas_call(
        functools.partial(_gdn_gates_kernel, t=t),
        grid=(b,),
        in_specs=[pl.BlockSpec((1, nrow, t), lambda i: (i, 0, 0)),
                  pl.BlockSpec((nrow, 1), lambda i: (0, 0)),
                  pl.BlockSpec((nrow, 1), lambda i: (0, 0))],
        out_specs=pl.BlockSpec((1, nrow, t), lambda i: (i, 0, 0)),
        out_shape=jax.ShapeDtypeStruct((b, nrow, t), F32),
        compiler_params=_cparams(("parallel",)),
        name="gdn_gates",
    )(ab_t, alog_col, dtb_col)


def _split2(x):
    hi = x.astype(BF16)
    return hi, (x - hi.astype(F32)).astype(BF16)


def _dot_split(a, b):
    ah, al = _split2(a)
    bh, bl = _split2(b)
    return _dot(ah, bh) + (_dot(al, bh) + _dot(ah, bl))


GDN_INV_BASE = 16


def _unit_triangular_inverse(m, ri, li):
    n = m.shape[0]
    s = GDN_INV_BASE
    inblk = (ri // s) == (li // s)
    p = jnp.where(inblk, -m, 0.0)
    q = p
    p = _dot_split(p, p)
    level = 2
    while 2 * level < s:
        both = _dot_split(jnp.concatenate([q, p], axis=0), p)
        q = q + p + both[:n]
        p = both[n:]
        level *= 2
    q = q + p + _dot_split(q, p)
    x = jnp.where(ri == li, 1.0, 0.0) + q
    while s < n:
        off = ((ri // (2 * s)) == (li // (2 * s))) & ((ri // s) != (li // s))
        cx = _dot_split(jnp.where(off, m, 0.0), x)
        x = x - _dot_split(x, cx)
        s *= 2
    return x


def _gdn_kernel(gtc_ref, gtl_ref, zqc_ref, zkc_ref, zvc_ref, zgc_ref, zql_ref, zkl_ref, zvl_ref, zgl_ref,
                cwq_ref, cwk_ref, cwv_ref, nw_ref, ac_ref, al_ref,
                pad_ref, q_s, k_s, v_s, gcol_s, u_s, w_s, qg_s, kd_s, a_s, egl_s, o_s, st_s, *, tc, tl):
    c = GDN_CHUNK
    hd = GDN_HEAD_DIM
    ttot = tc + tl
    rt = 256
    halo = 8
    kw = GDN_CONV
    lead = halo - (kw - 1) // 2

    def conv_seg(z_ref, cw_ref, n, width, dst_ref, base, l2_scale):
        zero = jnp.zeros((halo, width), F32)
        pad_ref[0:halo, 0:width] = zero
        pad_ref[halo + n:2 * halo + n, 0:width] = zero

        def fill(i, carry):
            r0 = pl.multiple_of(i * rt, rt)
            pad_ref[pl.ds(halo + r0, rt), 0:width] = z_ref[pl.ds(r0, rt), :].astype(F32)
            return carry
        lax.fori_loop(0, n // rt, fill, 0)
        cw = cw_ref[...]

        def body(i, carry):
            r0 = pl.multiple_of(i * rt, rt)
            xx = pad_ref[pl.ds(r0, rt + 2 * halo), 0:width]
            acc = xx[lead:lead + rt, :] * cw[0:1, :]
            for k in range(1, kw):
                acc = acc + xx[lead + k:lead + k + rt, :] * cw[k:k + 1, :]
            y = _silu(acc)
            if l2_scale is not None:
                for j in range(width // hd):
                    yj = y[:, j * hd:(j + 1) * hd]
                    yj = yj * lax.rsqrt(jnp.sum(yj * yj, axis=-1, keepdims=True) + L2_EPS) * l2_scale
                    dst_ref[pl.ds(base + r0, rt), j * hd:(j + 1) * hd] = yj
            else:
                dst_ref[pl.ds(base + r0, rt), :] = y
            return carry
        lax.fori_loop(0, n // rt, body, 0)

    for z_ref, n, base in ((zqc_ref, tc, 0), (zql_ref, tl, tc)):
        conv_seg(z_ref, cwq_ref, n, hd, q_s, base, hd ** -0.5)
    for z_ref, n, base in ((zkc_ref, tc, 0), (zkl_ref, tl, tc)):
        conv_seg(z_ref, cwk_ref, n, hd, k_s, base, 1.0)
    for z_ref, n, base in ((zvc_ref, tc, 0), (zvl_ref, tl, tc)):
        conv_seg(z_ref, cwv_ref, n, 2 * hd, v_s, base, None)

    def gate_cols(gt_ref, n, base):
        def body(i, carry):
            r0 = pl.multiple_of(i * c, c)
            tile = gt_ref[0, :, pl.ds(r0, c)]
            gcol_s[pl.ds(base + r0, c), :] = jnp.concatenate([tile] * (c // 8), axis=0).T
            return carry
        lax.fori_loop(0, n // c, body, 0)

    gate_cols(gtc_ref, tc, 0)
    gate_cols(gtl_ref, tl, tc)

    ri = lax.broadcasted_iota(jnp.int32, (c, c), 0)
    li = lax.broadcasted_iota(jnp.int32, (c, c), 1)

    def prep(gt_ref, n, base):
        def body(i, carry):
            r0 = pl.multiple_of(i * c, c)
            row = base + r0
            g8 = gt_ref[0, :, pl.ds(r0, c)]
            gc8 = gcol_s[pl.ds(row, c), :]
            kb = k_s[pl.ds(row, c), :]
            qb = q_s[pl.ds(row, c), :]
            kbb = kb.astype(BF16)
            kk = _dot_nt(kbb, kbb)
            qk = _dot_nt(qb.astype(BF16), kbb)
            for d in range(2):
                incl = (ri >= li) if d == 0 else (ri <= li)
                strict = (ri > li) if d == 0 else (ri < li)
                for r in range(2):
                    ch = 2 * d + r
                    g_row = g8[ch:ch + 1, :]
                    g_col = gc8[:, ch:ch + 1]
                    beta = gc8[:, 4 + ch:5 + ch]
                    decay = jnp.exp(jnp.where(incl, g_col - g_row, -jnp.inf))
                    mmat = jnp.where(strict, kk * decay, 0.0) * beta
                    tinv = _unit_triangular_inverse(mmat, ri, li)
                    eg = jnp.exp(g_col)
                    vb = v_s[pl.ds(row, c), r * hd:(r + 1) * hd] * beta
                    kbe = kb * (beta * eg)
                    sol = _dot_split(tinv, jnp.concatenate([vb, kbe], axis=1))
                    u_s[ch, pl.ds(row, c), :] = sol[:, :hd]
                    w_s[ch, pl.ds(row, c), :] = sol[:, hd:].astype(BF16)
                    qg_s[ch, pl.ds(row, c), :] = (qb * eg).astype(BF16)
                    a_s[ch, pl.ds(row, c), :] = (qk * decay).astype(BF16)
                    g_last = g_row[:, c - 1:c] if d == 0 else g_row[:, 0:1]
                    kd_s[ch, pl.ds(row, c), :] = (kb * jnp.exp(g_last - g_col)).astype(BF16)
                    egl_s[ch, pl.ds(base // c + i, 1), :] = jnp.broadcast_to(jnp.exp(g_last), (1, hd))
            return carry
        lax.fori_loop(0, n // c, body, 0)

    prep(gtc_ref, tc, 0)
    prep(gtl_ref, tl, tc)

    st_s[...] = jnp.zeros_like(st_s)

    def scan(n, base):
        def body(i, carry):
            for ch in range(4):
                idx = i if ch < 2 else n // c - 1 - i
                row = pl.multiple_of(base + idx * c, c)
                s = st_s[ch]
                sb = s.astype(BF16)
                vnew = u_s[ch, pl.ds(row, c), :] - _dot(w_s[ch, pl.ds(row, c), :], sb)
                vnb = vnew.astype(BF16)
                o_s[ch, pl.ds(row, c), :] = (_dot(qg_s[ch, pl.ds(row, c), :], sb)
                                             + _dot(a_s[ch, pl.ds(row, c), :], vnb))
                egl = egl_s[ch, pl.ds(base // c + idx, 1), :]
                st_s[ch] = s * egl + _dot_tn(kd_s[ch, pl.ds(row, c), :], vnb)
            return carry
        lax.fori_loop(0, n // c, body, 0)

    scan(tc, 0)
    scan(tl, tc)

    nw = nw_ref[...]

    def finish(zg_ref, a_ref, n, base):
        def body(i, carry):
            r0 = pl.multiple_of(i * rt, rt)
            for r in range(2):
                y = o_s[r, pl.ds(base + r0, rt), :] + o_s[2 + r, pl.ds(base + r0, rt), :]
                yn = y * lax.rsqrt(jnp.mean(y * y, axis=-1, keepdims=True) + NORM_EPS) * nw
                g = zg_ref[pl.ds(r0, rt), r * hd:(r + 1) * hd].astype(F32)
                a_ref[pl.ds(r0, rt), r * hd:(r + 1) * hd] = (yn * _silu(g)).astype(BF16)
            return carry
        lax.fori_loop(0, n // rt, body, 0)

    finish(zgc_ref, ac_ref, tc, 0)
    finish(zgl_ref, al_ref, tl, tc)


def gdn_mixer(zc, zl, gtc, gtl, batch, conv_w, norm_w):
    tc = zc.shape[0] // batch
    tl = zl.shape[0] // batch
    hd = GDN_HEAD_DIM
    nqk = zc.shape[1] // hd // 6
    ttot = tc + tl
    kern = functools.partial(_gdn_kernel, tc=tc, tl=tl)

    def seg_specs(t):
        return [pl.BlockSpec((t, hd), lambda b, j: (b, j)),
                pl.BlockSpec((t, hd), lambda b, j: (b, nqk + j)),
                pl.BlockSpec((t, 2 * hd), lambda b, j: (b, nqk + j)),
                pl.BlockSpec((t, 2 * hd), lambda b, j: (b, 2 * nqk + j))]

    kw = conv_w.shape[0]
    ac, al = pl.pallas_call(
        kern,
        grid=(batch, nqk),
        in_specs=[pl.BlockSpec((1, 8, tc), lambda b, j: (b, j, 0)),
                  pl.BlockSpec((1, 8, tl), lambda b, j: (b, j, 0))]
                 + seg_specs(tc) + seg_specs(tl)
                 + [pl.BlockSpec((kw, hd), lambda b, j: (0, j)),
                    pl.BlockSpec((kw, hd), lambda b, j: (0, nqk + j)),
                    pl.BlockSpec((kw, 2 * hd), lambda b, j: (0, nqk + j)),
                    pl.BlockSpec((1, hd), lambda b, j: (0, 0))],
        out_specs=[pl.BlockSpec((tc, 2 * hd), lambda b, j: (b, j)),
                   pl.BlockSpec((tl, 2 * hd), lambda b, j: (b, j))],
        out_shape=[jax.ShapeDtypeStruct((batch * tc, 2 * nqk * hd), BF16),
                   jax.ShapeDtypeStruct((batch * tl, 2 * nqk * hd), BF16)],
        scratch_shapes=[pltpu.VMEM((max(tc, tl) + 16, 2 * hd), F32),
                        pltpu.VMEM((ttot, hd), F32), pltpu.VMEM((ttot, hd), F32),
                        pltpu.VMEM((ttot, 2 * hd), F32),
                        pltpu.VMEM((ttot, 128), F32),
                        pltpu.VMEM((4, ttot, hd), F32),
                        pltpu.VMEM((4, ttot, hd), BF16),
                        pltpu.VMEM((4, ttot, hd), BF16),
                        pltpu.VMEM((4, ttot, hd), BF16),
                        pltpu.VMEM((4, ttot, GDN_CHUNK), BF16),
                        pltpu.VMEM((4, ttot // GDN_CHUNK, hd), F32),
                        pltpu.VMEM((4, ttot, hd), F32),
                        pltpu.VMEM((4, hd, hd), F32)],
        compiler_params=_cparams(("parallel", "parallel")),
        name="gdn_mixer",
    )(gtc, gtl, zc, zc, zc, zc, zl, zl, zl, zl, conv_w, conv_w, conv_w, norm_w.reshape(1, hd))
    return al, ac


def gdn_gate_weights(w_ab, a_log, dt_bias):
    nv = a_log.shape[-1]
    nqk = nv // 2
    cols, alog, dtb = [], [], []
    for j in range(nqk):
        for kind in range(2):
            for d in range(2):
                for r in range(2):
                    hv = 2 * j + r
                    cols.append(w_ab[d][:, kind * nv + hv])
                    alog.append(a_log[d, hv] if kind == 0 else jnp.zeros((), F32))
                    dtb.append(dt_bias[d, hv] if kind == 0 else jnp.zeros((), F32))
    return jnp.stack(cols, axis=1), jnp.stack(alog)[:, None], jnp.stack(dtb)[:, None]


def _na_cases(rows):
    kh = min(WIN_H, rows)
    nbk = min(NA_KROWS, rows)
    cases, case_of_block, kstart = [], [], []
    for i in range(rows // NA_QROWS):
        r0 = i * NA_QROWS
        b0 = int(np.clip(r0 - kh // 2, 0, rows - nbk))
        sig = []
        for qr in range(NA_QROWS):
            q_row = r0 + qr
            row_start = int(np.clip(q_row - kh // 2, 0, rows - kh))
            for kr in range(nbk):
                k_row = b0 + kr
                ok = row_start <= k_row < row_start + kh
                dr = int(np.clip(k_row - q_row + WIN_H - 1, 0, 2 * WIN_H - 2))
                sig.append((ok, dr))
        sig = tuple(sig)
        if sig not in cases:
            cases.append(sig)
        case_of_block.append(cases.index(sig))
        kstart.append(b0 * GRID_W)
    return cases, case_of_block, kstart, nbk


def _na_bias_kernel(rpb_ref, o_ref, *, cases, nbk):
    h = pl.program_id(0)
    ndr, ndc = 2 * WIN_H - 1, 2 * WIN_W - 1
    kwid = min(WIN_W, GRID_W)
    qc = lax.broadcasted_iota(jnp.int32, (GRID_W, 128), 0)
    lane = lax.broadcasted_iota(jnp.int32, (GRID_W, 128), 1)
    kc = lane % GRID_W
    second = (lane // GRID_W).astype(F32)
    col_start = jnp.clip(qc - kwid // 2, 0, GRID_W - kwid)
    col_ok = (kc >= col_start) & (kc < col_start + kwid)
    dc = jnp.clip(kc - qc + WIN_W - 1, 0, ndc - 1)
    base = h * (ndr * ndc)
    cache = {}

    def tile_for(dr0, dr1):
        key = (dr0, dr1)
        if key not in cache:
            val = jnp.zeros((GRID_W, 128), F32)
            for n in range(ndc):
                s0 = rpb_ref[base + dr0 * ndc + n]
                s1 = rpb_ref[base + dr1 * ndc + n]
                val = jnp.where(dc == n, s0 + (s1 - s0) * second, val)
            cache[key] = val
        return cache[key]

    for ci, sig in enumerate(cases):
        for qr in range(NA_QROWS):
            for lt in range(nbk // 2):
                ok0, dr0 = sig[qr * nbk + 2 * lt]
                ok1, dr1 = sig[qr * nbk + 2 * lt + 1]
                if not (ok0 or ok1):
                    tile = jnp.full((GRID_W, 128), NEG_BIG, F32)
                else:
                    row_pen = (float(ok0) - 1.0) + (float(ok1) - float(ok0)) * second
                    tile = jnp.where(col_ok, tile_for(dr0, dr1), NEG_BIG) - row_pen * NEG_BIG
                o_ref[0, ci, qr * GRID_W:(qr + 1) * GRID_W, lt * 128:(lt + 1) * 128] = tile


def na_bias_tables(rpb, cases, nbk):
    nh = rpb.shape[0]
    ncase = len(cases)
    qblk = NA_QROWS * GRID_W
    return pl.pallas_call(
        functools.partial(_na_bias_kernel, cases=cases, nbk=nbk),
        grid=(nh,),
        in_specs=[pl.BlockSpec(memory_space=pltpu.SMEM)],
        out_specs=pl.BlockSpec((1, ncase, qblk, nbk * GRID_W), lambda h: (h, 0, 0, 0)),
        out_shape=jax.ShapeDtypeStruct((nh, ncase, qblk, nbk * GRID_W), F32),
        compiler_params=_cparams(("arbitrary",)),
        name="na_bias_tables",
    )(rpb.reshape(-1))


def _na_kernel(case_ref, kstart_ref, q_ref, k_ref, v_ref, qc_ref, kc_ref, vc_ref, bias_ref,
               ol_ref, oc_ref, *, nloc, scale):
    i = pl.program_id(2)
    kc = kc_ref[...]
    vc = vc_ref[...]

    @pl.when(i == 0)
    def _():
        s = _dot_nt(qc_ref[...], kc) * scale
        p = jnp.exp(s - jnp.max(s, axis=-1, keepdims=True))
        o = _dot(p.astype(BF16), vc) / jnp.sum(p, axis=-1, keepdims=True)
        oc_ref[...] = o.astype(BF16)

    q = q_ref[...]
    ks = pl.multiple_of(kstart_ref[i], GRID_W)
    kl = k_ref[pl.ds(ks, nloc), :]
    vl = v_ref[pl.ds(ks, nloc), :]
    s_loc = _dot_nt(q, kl) * scale + bias_ref[0, case_ref[i]]
    s_ctx = _dot_nt(q, kc) * scale
    m = jnp.maximum(jnp.max(s_loc, axis=-1, keepdims=True), jnp.max(s_ctx, axis=-1, keepdims=True))
    p_loc = jnp.exp(s_loc - m)
    p_ctx = jnp.exp(s_ctx - m)
    denom = jnp.sum(p_loc, axis=-1, keepdims=True) + jnp.sum(p_ctx, axis=-1, keepdims=True)
    o = (_dot(p_loc.astype(BF16), vl) + _dot(p_ctx.astype(BF16), vc)) / denom
    ol_ref[...] = o.astype(BF16)


def na_mixer(zc, zl, batch, rpb):
    tc = zc.shape[0] // batch
    tl = zl.shape[0] // batch
    d = zc.shape[1] // 3
    nh = NA_HEADS
    hd = d // nh
    rows = tl // GRID_W
    cases, case_of_block, kstart, nbk = _na_cases(rows)
    bias = na_bias_tables(rpb, cases, nbk)
    qblk = NA_QROWS * GRID_W
    nblk = rows // NA_QROWS
    nloc = nbk * GRID_W
    grid_spec = pltpu.PrefetchScalarGridSpec(
        num_scalar_prefetch=2,
        grid=(nh, batch, nblk),
        in_specs=[pl.BlockSpec((qblk, hd), lambda h, b, i, *_: (b * nblk + i, h)),
                  pl.BlockSpec((tl, hd), lambda h, b, i, *_: (b, nh + h)),
                  pl.BlockSpec((tl, hd), lambda h, b, i, *_: (b, 2 * nh + h)),
                  pl.BlockSpec((tc, hd), lambda h, b, i, *_: (b, h)),
                  pl.BlockSpec((tc, hd), lambda h, b, i, *_: (b, nh + h)),
                  pl.BlockSpec((tc, hd), lambda h, b, i, *_: (b, 2 * nh + h)),
                  pl.BlockSpec((1, len(cases), qblk, nloc), lambda h, b, i, *_: (h, 0, 0, 0))],
        out_specs=[pl.BlockSpec((qblk, hd), lambda h, b, i, *_: (b * nblk + i, h)),
                   pl.BlockSpec((tc, hd), lambda h, b, i, *_: (b, h))])
    ol, oc = pl.pallas_call(
        functools.partial(_na_kernel, nloc=nloc, scale=hd ** -0.5),
        grid_spec=grid_spec,
        out_shape=[jax.ShapeDtypeStruct((batch * tl, d), BF16),
                   jax.ShapeDtypeStruct((batch * tc, d), BF16)],
        compiler_params=_cparams(("parallel", "parallel", "arbitrary")),
        name="na_mixer",
    )(jnp.asarray(case_of_block, jnp.int32), jnp.asarray(kstart, jnp.int32),
      zl, zl, zl, zc, zc, zc, bias)
    return ol, oc


TM = 512
TN_PROJ = 1024
TK_OUT = 1024
TF_MLP = 1024


def kernel(x, c, ctx, c_ctx, ada_w, ada_b, norm_g, mlp_w1, mlp_w2, ret_w_in, ret_w_out, gdn_w_in, gdn_conv_w, gdn_w_ab, gdn_a_log, gdn_dt_bias, gdn_norm_w, gdn_w_out, na_w_in, na_w_out, na_rpb):
    batch, t, d = x.shape
    tctx = ctx.shape[1]
    depth = ada_w.shape[0]

    nrow = -(-(batch + 1) // 16) * 16
    cvec = jnp.zeros((nrow, d), F32).at[0].set(c_ctx).at[1:batch + 1].set(c)
    mods = ada_modulation(cvec, ada_w, ada_b).reshape(depth, nrow, N_MOD, d)

    xl = x.reshape(batch * t, d)
    xc = ctx.reshape(batch * tctx, d)
    ret_lg, ret_cos, ret_sin = retention_tables(t, ret_w_in.shape[2] // 6 // RET_HEADS)

    i_ret = i_gdn = i_na = 0
    for i in range(depth):
        last = i == depth - 1
        m_ctx = mods[i, 0:1]
        m_lat = mods[i, 1:batch + 1]
        g = norm_g[i]
        kind = i % 3

        def project(w, out_dtype=BF16, tn=TN_PROJ):
            zc = norm_mod_matmul(xc, m_ctx, batch * tctx, g[0], w, 0, 1, out_dtype, TM, tn)
            zl = norm_mod_matmul(xl, m_lat, t, g[0], w, 0, 1, out_dtype, TM, tn)
            return zc, zl

        if kind == 0:
            zc, zl = project(ret_w_in[i_ret].astype(BF16))
            al, ac = retention_mixer(zc, zl, batch, ret_lg, ret_cos, ret_sin, not last)
            w_out = ret_w_out[i_ret].astype(BF16)
            i_ret += 1
        elif kind == 1:
            zc, zl = project(gdn_w_in[i_gdn].astype(BF16))
            w_gate, alog_col, dtb_col = gdn_gate_weights(gdn_w_ab[i_gdn], gdn_a_log[i_gdn],
                                                        gdn_dt_bias[i_gdn])
            abc, abl = project(w_gate.astype(BF16), out_dtype=F32)
            gtc = gdn_gates(abc.reshape(batch, tctx, -1).transpose(0, 2, 1), alog_col, dtb_col)
            gtl = gdn_gates(abl.reshape(batch, t, -1).transpose(0, 2, 1), alog_col, dtb_col)
            al, ac = gdn_mixer(zc, zl, gtc, gtl, batch, gdn_conv_w[i_gdn], gdn_norm_w[i_gdn])
            w_out = gdn_w_out[i_gdn].astype(BF16)
            i_gdn += 1
        else:
            zc, zl = project(na_w_in[i_na].astype(BF16))
            al, ac = na_mixer(zc, zl, batch, na_rpb[i_na])
            w_out = na_w_out[i_na].astype(BF16)
            i_na += 1

        w1 = mlp_w1[i].astype(BF16)
        w2 = mlp_w2[i].astype(BF16)
        xl = out_proj_residual(al, w_out, xl, m_lat, t, g[1], 2, TM, TK_OUT)
        xl = mlp_residual(xl, m_lat, t, g[2], g[3], w1, w2, TM, TF_MLP)
        if not last:
            xc = out_proj_residual(ac, w_out, xc, m_ctx, batch * tctx, g[1], 2, TM, TK_OUT)
            xc = mlp_residual(xc, m_ctx, batch * tctx, g[2], g[3], w1, w2, TM, TF_MLP)
    return xl.reshape(batch, t, d)
```

```python
import functools
import math

import numpy as np
import jax
import jax.numpy as jnp
from jax import lax
from jax.experimental import pallas as pl
from jax.experimental.pallas import tpu as pltpu

F32 = jnp.float32
BF16 = jnp.bfloat16

NORM_EPS = 1e-6
L2_EPS = 1e-6
ROPE_BASE = 10000.0
N_MOD = 6
GRID_W = 64

RET_HEADS = 8
RET_CHUNK = 128
GDN_HEAD_DIM = 128
GDN_CHUNK = 128
GDN_CONV = 5
NA_HEADS = 16
WIN_H = 8
WIN_W = 16
NA_QROWS = 2
NA_KROWS = 10
NEG_BIG = -1e30
NA_GROUP = 4

V7X_VMEM_LIMIT_BYTES = 56 * 1024 * 1024


def _cparams(sem):
    return pltpu.CompilerParams(dimension_semantics=sem, vmem_limit_bytes=V7X_VMEM_LIMIT_BYTES)


def _sigmoid(x):
    return 1.0 / (1.0 + jnp.exp(-x))


def _silu(x):
    return x * _sigmoid(x)


def _dot(a, b):
    return jnp.dot(a, b, preferred_element_type=F32)


def _dot_nt(a, b):
    return lax.dot_general(a, b, (((1,), (1,)), ((), ())), preferred_element_type=F32)


def _dot_tn(a, b):
    return lax.dot_general(a, b, (((0,), (0,)), ((), ())), preferred_element_type=F32)


def _rms(y, g):
    return y * lax.rsqrt(jnp.mean(y * y, axis=-1, keepdims=True) + NORM_EPS) * g


def _ada_kernel(c_ref, w_ref, b_ref, o_ref):
    s = _silu(c_ref[...]).astype(BF16)
    o_ref[0] = _dot(s, w_ref[0].astype(BF16)) + b_ref[0]


def ada_modulation(cvec, ada_w, ada_b, tn=1024):
    depth, d, n = ada_w.shape
    r = cvec.shape[0]
    return pl.pallas_call(
        _ada_kernel,
        grid=(depth, n // tn),
        in_specs=[pl.BlockSpec((r, d), lambda l, j: (0, 0)),
                  pl.BlockSpec((1, d, tn), lambda l, j: (l, 0, j)),
                  pl.BlockSpec((1, 1, tn), lambda l, j: (l, 0, j))],
        out_specs=pl.BlockSpec((1, r, tn), lambda l, j: (l, 0, j)),
        out_shape=jax.ShapeDtypeStruct((depth, r, n), F32),
        compiler_params=_cparams(("parallel", "parallel")),
        name="ada_modulation",
    )(cvec, ada_w, ada_b.reshape(depth, 1, n))


def _modulated_norm(x, g, m_ref, shift_idx, scale_idx):
    y = _rms(x, g)
    return y * (1.0 + m_ref[0, scale_idx:scale_idx + 1, :]) + m_ref[0, shift_idx:shift_idx + 1, :]


def _nmm_kernel(x_ref, m_ref, g_ref, w_ref, o_ref, h_ref, *, shift_idx, scale_idx):
    @pl.when(pl.program_id(1) == 0)
    def _():
        h_ref[...] = _modulated_norm(x_ref[...], g_ref[...], m_ref, shift_idx, scale_idx).astype(BF16)

    o_ref[...] = _dot(h_ref[...], w_ref[...]).astype(o_ref.dtype)


def norm_mod_matmul(x, mods, tokens_per_mod, g, w, shift_idx, scale_idx, out_dtype, tm, tn):
    n, d = x.shape
    nout = w.shape[1]
    tm = min(tm, tokens_per_mod)
    tn = min(tn, nout)
    bpm = tokens_per_mod // tm
    return pl.pallas_call(
        functools.partial(_nmm_kernel, shift_idx=shift_idx, scale_idx=scale_idx),
        grid=(n // tm, nout // tn),
        in_specs=[pl.BlockSpec((tm, d), lambda i, j: (i, 0)),
                  pl.BlockSpec((1, N_MOD, d), lambda i, j: (i // bpm, 0, 0)),
                  pl.BlockSpec((1, d), lambda i, j: (0, 0)),
                  pl.BlockSpec((d, tn), lambda i, j: (0, j))],
        out_specs=pl.BlockSpec((tm, tn), lambda i, j: (i, j)),
        out_shape=jax.ShapeDtypeStruct((n, nout), out_dtype),
        scratch_shapes=[pltpu.VMEM((tm, d), BF16)],
        compiler_params=_cparams(("parallel", "arbitrary")),
        name="norm_mod_matmul",
    )(x, mods, g.reshape(1, d), w)


def _out_res_kernel(a_ref, w_ref, x_ref, m_ref, g_ref, o_ref, acc_ref, *, gate_idx, nk):
    k = pl.program_id(1)

    @pl.when(k == 0)
    def _():
        acc_ref[...] = jnp.zeros_like(acc_ref)

    acc_ref[...] += _dot(a_ref[...], w_ref[...])

    @pl.when(k == nk - 1)
    def _():
        yn = _rms(acc_ref[...], g_ref[...])
        o_ref[...] = x_ref[...] + m_ref[0, gate_idx:gate_idx + 1, :] * yn


def out_proj_residual(a, w, x, mods, tokens_per_mod, g, gate_idx, tm, tk):
    n, kdim = a.shape
    d = w.shape[1]
    tm = min(tm, tokens_per_mod)
    tk = min(tk, kdim)
    bpm = tokens_per_mod // tm
    nk = kdim // tk
    return pl.pallas_call(
        functools.partial(_out_res_kernel, gate_idx=gate_idx, nk=nk),
        grid=(n // tm, nk),
        in_specs=[pl.BlockSpec((tm, tk), lambda i, k: (i, k)),
                  pl.BlockSpec((tk, d), lambda i, k: (k, 0)),
                  pl.BlockSpec((tm, d), lambda i, k: (i, 0)),
                  pl.BlockSpec((1, N_MOD, d), lambda i, k: (i // bpm, 0, 0)),
                  pl.BlockSpec((1, d), lambda i, k: (0, 0))],
        out_specs=pl.BlockSpec((tm, d), lambda i, k: (i, 0)),
        out_shape=jax.ShapeDtypeStruct((n, d), F32),
        scratch_shapes=[pltpu.VMEM((tm, d), F32)],
        compiler_params=_cparams(("parallel", "arbitrary")),
        name="out_proj_residual",
    )(a, w, x, mods, g.reshape(1, d))


def _mlp_kernel(x_ref, m_ref, g2_ref, g3_ref, w1_ref, w2_ref, o_ref, h_ref, acc_ref, *, nk):
    k = pl.program_id(1)

    @pl.when(k == 0)
    def _():
        h_ref[...] = _modulated_norm(x_ref[...], g2_ref[...], m_ref, 3, 4).astype(BF16)
        acc_ref[...] = jnp.zeros_like(acc_ref)

    a = jnp.maximum(_dot(h_ref[...], w1_ref[...]), 0.0)
    acc_ref[...] += _dot((a * a).astype(BF16), w2_ref[...])

    @pl.when(k == nk - 1)
    def _():
        yn = _rms(acc_ref[...], g3_ref[...])
        o_ref[...] = x_ref[...] + m_ref[0, 5:6, :] * yn


def mlp_residual(x, mods, tokens_per_mod, g2, g3, w1, w2, tm, tf):
    n, d = x.shape
    dff = w1.shape[1]
    tm = min(tm, tokens_per_mod)
    tf = min(tf, dff)
    bpm = tokens_per_mod // tm
    nk = dff // tf
    return pl.pallas_call(
        functools.partial(_mlp_kernel, nk=nk),
        grid=(n // tm, nk),
        in_specs=[pl.BlockSpec((tm, d), lambda i, k: (i, 0)),
                  pl.BlockSpec((1, N_MOD, d), lambda i, k: (i // bpm, 0, 0)),
                  pl.BlockSpec((1, d), lambda i, k: (0, 0)),
                  pl.BlockSpec((1, d), lambda i, k: (0, 0)),
                  pl.BlockSpec((d, tf), lambda i, k: (0, k)),
                  pl.BlockSpec((tf, d), lambda i, k: (k, 0))],
        out_specs=pl.BlockSpec((tm, d), lambda i, k: (i, 0)),
        out_shape=jax.ShapeDtypeStruct((n, d), F32),
        scratch_shapes=[pltpu.VMEM((tm, d), BF16), pltpu.VMEM((tm, d), F32)],
        compiler_params=_cparams(("parallel", "arbitrary")),
        name="mlp_residual",
    )(x, mods, g2.reshape(1, d), g3.reshape(1, d), w1, w2)


def _ret_kernel(lg_ref, cos_ref, sin_ref, qc_ref, kc_ref, vc_ref, gc_ref, ql_ref, kl_ref, vl_ref, gl_ref,
                *rest, tc, tl, dk, dv, with_ctx_out):
    if with_ctx_out:
        ac_ref, al_ref, qs_ref, ks_ref, of_ref, ob_ref, rf_ref, rb_ref = rest
    else:
        al_ref, qs_ref, ks_ref, of_ref, ob_ref, rf_ref, rb_ref = rest
        ac_ref = None
    c = RET_CHUNK
    half = dk // 2
    k_scale = dk ** -0.5

    qs_ref[0:tc, :] = qc_ref[...]
    ks_ref[0:tc, :] = (kc_ref[...].astype(F32) * k_scale).astype(BF16)

    rt = 256

    def rope_body(i, carry):
        r0 = pl.multiple_of(i * rt, rt)
        cs = cos_ref[pl.ds(r0, rt), :]
        sn = sin_ref[pl.ds(r0, rt), :]
        for src, dst, scale in ((ql_ref, qs_ref, 1.0), (kl_ref, ks_ref, k_scale)):
            t = src[pl.ds(r0, rt), :].astype(F32) * scale
            t1, t2 = t[:, :half], t[:, half:]
            dst[pl.ds(tc + r0, rt), :] = jnp.concatenate(
                [t1 * cs - t2 * sn, t1 * sn + t2 * cs], axis=-1).astype(BF16)
        return carry

    lax.fori_loop(0, tl // rt, rope_body, 0)

    lgf = lg_ref[0, 0:1, 0:1]
    lgb = lg_ref[0, 1:2, 0:1]
    ci = lax.broadcasted_iota(jnp.int32, (c, c), 0)
    mi = lax.broadcasted_iota(jnp.int32, (c, c), 1)
    dist = (ci - mi).astype(F32)
    intra_f = jnp.where(dist >= 0, jnp.exp(lgf * jnp.maximum(dist, 0.0)), 0.0)
    intra_b = jnp.where(dist <= 0, jnp.exp(lgb * jnp.maximum(-dist, 0.0)), 0.0)
    pos = lax.broadcasted_iota(jnp.int32, (c, 1), 0).astype(F32)
    xi_f = jnp.exp(lgf * (pos + 1.0))
    zeta_f = jnp.exp(lgf * (c - 1.0 - pos))
    xi_b = jnp.exp(lgb * (c - pos))
    zeta_b = jnp.exp(lgb * pos)
    cd_f = jnp.exp(lgf * float(c))
    cd_b = jnp.exp(lgb * float(c))

    rf_ref[...] = jnp.zeros_like(rf_ref)
    rb_ref[...] = jnp.zeros_like(rb_ref)
    dirs = ((rf_ref, of_ref, intra_f, xi_f, zeta_f, cd_f), (rb_ref, ob_ref, intra_b, xi_b, zeta_b, cd_b))

    def make_body(v_ref, base, n):
        def body(i, carry):
            loc = (pl.multiple_of(i * c, c), pl.multiple_of((n - 1 - i) * c, c))
            qt = [qs_ref[pl.ds(base + l, c), :] for l in loc]
            kt = [ks_ref[pl.ds(base + l, c), :] for l in loc]
            vt = [v_ref[pl.ds(l, c), :] for l in loc]
            r = [dr[0][...] for dr in dirs]
            s = [_dot_nt(q, k) for q, k in zip(qt, kt)]
            qr = [_dot(q, x.astype(BF16)) for q, x in zip(qt, r)]
            kv = [_dot_tn(k, (v.astype(F32) * dr[4]).astype(BF16)) for k, v, dr in zip(kt, vt, dirs)]
            sv = [_dot((x * dr[2]).astype(BF16), v) for x, v, dr in zip(s, vt, dirs)]
            for d, dr in enumerate(dirs):
                dr[1][pl.ds(base + loc[d], c), :] = sv[d] + qr[d] * dr[3]
                dr[0][...] = r[d] * dr[5] + kv[d]
            return carry
        return body

    lax.fori_loop(0, tc // c, make_body(vc_ref, 0, tc // c), 0)
    lax.fori_loop(0, tl // c, make_body(vl_ref, tc, tl // c), 0)

    def finish(g_ref, a_ref, base, n):
        def body(i, carry):
            r0 = pl.multiple_of(i * rt, rt)
            y = of_ref[pl.ds(base + r0, rt), :] + ob_ref[pl.ds(base + r0, rt), :]
            yn = y * lax.rsqrt(jnp.mean(y * y, axis=-1, keepdims=True) + NORM_EPS)
            g = g_ref[pl.ds(r0, rt), :].astype(F32)
            a_ref[pl.ds(r0, rt), :] = (_silu(g) * yn).astype(BF16)
            return carry
        lax.fori_loop(0, n // rt, body, 0)

    if with_ctx_out:
        finish(gc_ref, ac_ref, 0, tc)
    finish(gl_ref, al_ref, tc, tl)


def retention_mixer(zc, zl, batch, lg, cos, sin, with_ctx_out):
    tc = zc.shape[0] // batch
    tl = zl.shape[0] // batch
    width = zc.shape[1]
    dk = width // 6 // RET_HEADS
    dv = 2 * dk
    h = RET_HEADS
    kern = functools.partial(_ret_kernel, tc=tc, tl=tl, dk=dk, dv=dv, with_ctx_out=with_ctx_out)

    def seg_specs(t):
        return [pl.BlockSpec((t, dk), lambda b, j: (b, j)),
                pl.BlockSpec((t, dk), lambda b, j: (b, h + j)),
                pl.BlockSpec((t, dv), lambda b, j: (b, h + j)),
                pl.BlockSpec((t, dv), lambda b, j: (b, 2 * h + j))]

    out_shape = [jax.ShapeDtypeStruct((batch * tl, h * dv), BF16)]
    out_specs = [pl.BlockSpec((tl, dv), lambda b, j: (b, j))]
    if with_ctx_out:
        out_shape = [jax.ShapeDtypeStruct((batch * tc, h * dv), BF16)] + out_shape
        out_specs = [pl.BlockSpec((tc, dv), lambda b, j: (b, j))] + out_specs
    outs = pl.pallas_call(
        kern,
        grid=(batch, h),
        in_specs=[pl.BlockSpec((1, 8, 128), lambda b, j: (j, 0, 0)),
                  pl.BlockSpec((tl, dk // 2), lambda b, j: (0, 0)),
                  pl.BlockSpec((tl, dk // 2), lambda b, j: (0, 0))] + seg_specs(tc) + seg_specs(tl),
        out_specs=out_specs,
        out_shape=out_shape,
        scratch_shapes=[pltpu.VMEM((tc + tl, dk), BF16), pltpu.VMEM((tc + tl, dk), BF16),
                        pltpu.VMEM((tc + tl, dv), F32), pltpu.VMEM((tc + tl, dv), F32),
                        pltpu.VMEM((dk, dv), F32), pltpu.VMEM((dk, dv), F32)],
        compiler_params=_cparams(("parallel", "parallel")),
        name="retention_mixer",
    )(lg, cos, sin, zc, zc, zc, zc, zl, zl, zl, zl)
    if with_ctx_out:
        return outs[1], outs[0]
    return outs[0], None


def retention_tables(t, dk):
    fwd = jnp.log(1.0 - 2.0 ** (-5.0 - jnp.arange(RET_HEADS, dtype=F32)))
    lg = jnp.zeros((RET_HEADS, 8, 128), F32)
    lg = lg.at[:, 0, :].set(fwd[:, None]).at[:, 1, :].set(fwd[::-1][:, None])
    tt = jnp.arange(t)
    row = (tt // GRID_W).astype(F32)
    col = (tt % GRID_W).astype(F32)
    n_pairs = dk // 2
    inv = ROPE_BASE ** (-jnp.arange(0, n_pairs, 2, dtype=F32) / n_pairs)
    ang = jnp.concatenate([row[:, None] * inv, col[:, None] * inv], axis=-1)
    return lg, jnp.cos(ang), jnp.sin(ang)


def _split3(x):
    hi = x.astype(BF16)
    r1 = x - hi.astype(F32)
    mid = r1.astype(BF16)
    lo = (r1 - mid.astype(F32)).astype(BF16)
    return hi, mid, lo


def _gdn_gates_kernel(ab_ref, alog_ref, dtb_ref, o_ref, *, t):
    c = GDN_CHUNK
    nrow = ab_ref.shape[1]
    rowid = lax.broadcasted_iota(jnp.int32, (nrow, 1), 0) % 8
    mi = lax.broadcasted_iota(jnp.int32, (c, c), 0)
    ci = lax.broadcasted_iota(jnp.int32, (c, c), 1)
    pre = (mi <= ci).astype(BF16)
    suf = (mi >= ci).astype(BF16)
    a_neg = -jnp.exp(alog_ref[...])
    dtb = dtb_ref[...]
    for w in range(t // c):
        x = ab_ref[0, :, w * c:(w + 1) * c]
        z = x + dtb
        g = a_neg * (jnp.maximum(z, 0.0) + jnp.log1p(jnp.exp(-jnp.abs(z))))
        parts = _split3(g)
        cp = sum(_dot(p, pre) for p in parts)
        cs = sum(_dot(p, suf) for p in parts)
        o_ref[0, :, w * c:(w + 1) * c] = jnp.where(rowid < 2, cp, jnp.where(rowid < 4, cs, _sigmoid(x)))


def gdn_gates(ab_t, alog_col, dtb_col):
    b, nrow, t = ab_t.shape
    return pl.pallas_call(
        functools.partial(_gdn_gates_kernel, t=t),
        grid=(b,),
        in_specs=[pl.BlockSpec((1, nrow, t), lambda i: (i, 0, 0)),
                  pl.BlockSpec((nrow, 1), lambda i: (0, 0)),
                  pl.BlockSpec((nrow, 1), lambda i: (0, 0))],
        out_specs=pl.BlockSpec((1, nrow, t), lambda i: (i, 0, 0)),
        out_shape=jax.ShapeDtypeStruct((b, nrow, t), F32),
        compiler_params=_cparams(("parallel",)),
        name="gdn_gates",
    )(ab_t, alog_col, dtb_col)


def _dot_each(a_list, b_list):
    return [_dot(a.astype(BF16), b.astype(BF16)) for a, b in zip(a_list, b_list)]


GDN_INV_BASE = 16
GDN_PREP_CHUNKS = 2


def _unit_triangular_inverse_minus_eye(ms, ri, li):
    n = ms[0].shape[0]
    s = GDN_INV_BASE
    inblk = (ri // s) == (li // s)
    ps = [jnp.where(inblk, -m, 0.0) for m in ms]
    qs = ps
    ps = _dot_each(ps, ps)
    level = 2
    while 2 * level < s:
        both = _dot_each([jnp.concatenate([q, p], axis=0) for q, p in zip(qs, ps)], ps)
        qs = [q + p + b[:n] for q, p, b in zip(qs, ps, both)]
        ps = [b[n:] for b in both]
        level *= 2
    qp = _dot_each(qs, ps)
    eye = jnp.where(ri == li, 1.0, 0.0)
    xs = [eye + q + p + t for q, p, t in zip(qs, ps, qp)]
    while s < n:
        off = ((ri // (2 * s)) == (li // (2 * s))) & ((ri // s) != (li // s))
        cxs = _dot_each([jnp.where(off, m, 0.0) for m in ms], xs)
        xcx = _dot_each(xs, cxs)
        xs = [x - t for x, t in zip(xs, xcx)]
        s *= 2
    return [x - eye for x in xs]


def _gdn_kernel(gtc_ref, gtl_ref, zqc_ref, zkc_ref, zvc_ref, zgc_ref, zql_ref, zkl_ref, zvl_ref, zgl_ref,
                cwq_ref, cwk_ref, cwv_ref, nw_ref, ac_ref, al_ref,
                pad_ref, q_s, k_s, v_s, gcol_s, u_s, w_s, qg_s, kd_s, a_s, egl_s, o_s, st_s, *, tc, tl):
    c = GDN_CHUNK
    hd = GDN_HEAD_DIM
    ttot = tc + tl
    rt = 256
    halo = 8
    kw = GDN_CONV
    lead = halo - (kw - 1) // 2

    def conv_seg(z_ref, cw_ref, n, width, dst_ref, base, l2_scale):
        zero = jnp.zeros((halo, width), F32)
        pad_ref[0:halo, 0:width] = zero
        pad_ref[halo + n:2 * halo + n, 0:width] = zero

        def fill(i, carry):
            r0 = pl.multiple_of(i * rt, rt)
            pad_ref[pl.ds(halo + r0, rt), 0:width] = z_ref[pl.ds(r0, rt), :].astype(F32)
            return carry
        lax.fori_loop(0, n // rt, fill, 0)
        cw = cw_ref[...]

        def body(i, carry):
            r0 = pl.multiple_of(i * rt, rt)
            xx = pad_ref[pl.ds(r0, rt + 2 * halo), 0:width]
            acc = xx[lead:lead + rt, :] * cw[0:1, :]
            for k in range(1, kw):
                acc = acc + xx[lead + k:lead + k + rt, :] * cw[k:k + 1, :]
            y = _silu(acc)
            if l2_scale is not None:
                for j in range(width // hd):
                    yj = y[:, j * hd:(j + 1) * hd]
                    yj = yj * lax.rsqrt(jnp.sum(yj * yj, axis=-1, keepdims=True) + L2_EPS) * l2_scale
                    dst_ref[pl.ds(base + r0, rt), j * hd:(j + 1) * hd] = yj
            else:
                dst_ref[pl.ds(base + r0, rt), :] = y
            return carry
        lax.fori_loop(0, n // rt, body, 0)

    for z_ref, n, base in ((zqc_ref, tc, 0), (zql_ref, tl, tc)):
        conv_seg(z_ref, cwq_ref, n, hd, q_s, base, hd ** -0.5)
    for z_ref, n, base in ((zkc_ref, tc, 0), (zkl_ref, tl, tc)):
        conv_seg(z_ref, cwk_ref, n, hd, k_s, base, 1.0)
    for z_ref, n, base in ((zvc_ref, tc, 0), (zvl_ref, tl, tc)):
        conv_seg(z_ref, cwv_ref, n, 2 * hd, v_s, base, None)

    def gate_cols(gt_ref, n, base):
        def body(i, carry):
            r0 = pl.multiple_of(i * c, c)
            tile = gt_ref[0, :, pl.ds(r0, c)]
            gcol_s[pl.ds(base + r0, c), :] = jnp.concatenate([tile] * (c // 8), axis=0).T
            return carry
        lax.fori_loop(0, n // c, body, 0)

    gate_cols(gtc_ref, tc, 0)
    gate_cols(gtl_ref, tl, tc)

    ri = lax.broadcasted_iota(jnp.int32, (c, c), 0)
    li = lax.broadcasted_iota(jnp.int32, (c, c), 1)

    def prep(gt_ref, n, base, cpi):
        def body(i, carry):
            rows, chans, mmats = [], [], []
            for cc in range(cpi):
                r0 = pl.multiple_of((i * cpi + cc) * c, c)
                row = base + r0
                g8 = gt_ref[0, :, pl.ds(r0, c)]
                gc8 = gcol_s[pl.ds(row, c), :]
                kb = k_s[pl.ds(row, c), :]
                qb = q_s[pl.ds(row, c), :]
                kbb = kb.astype(BF16)
                kk = _dot_nt(kbb, kbb)
                qk = _dot_nt(qb.astype(BF16), kbb)
                for d in range(2):
                    incl = (ri >= li) if d == 0 else (ri <= li)
                    strict = (ri > li) if d == 0 else (ri < li)
                    for r in range(2):
                        ch = 2 * d + r
                        g_row = g8[ch:ch + 1, :]
                        g_col = gc8[:, ch:ch + 1]
                        beta = gc8[:, 4 + ch:5 + ch]
                        decay = jnp.exp(jnp.where(incl, g_col - g_row, -jnp.inf))
                        mmats.append(jnp.where(strict, kk * decay, 0.0) * beta)
                        rows.append(row)
                        chans.append(ch)
                        qg_s[ch, pl.ds(row, c), :] = (qb * jnp.exp(g_col)).astype(BF16)
                        a_s[ch, pl.ds(row, c), :] = (qk * decay).astype(BF16)
                        g_last = g_row[:, c - 1:c] if d == 0 else g_row[:, 0:1]
                        kd_s[ch, pl.ds(row, c), :] = (kb * jnp.exp(g_last - g_col)).astype(BF16)
                        egl_s[ch, pl.ds(base // c + i * cpi + cc, 1), :] = jnp.broadcast_to(
                            jnp.exp(g_last), (1, hd))
            tm1s = _unit_triangular_inverse_minus_eye(mmats, ri, li)
            rhss = []
            for row, ch in zip(rows, chans):
                gc8 = gcol_s[pl.ds(row, c), :]
                beta = gc8[:, 4 + ch:5 + ch]
                vb = v_s[pl.ds(row, c), (ch % 2) * hd:(ch % 2 + 1) * hd] * beta
                kbe = k_s[pl.ds(row, c), :] * (beta * jnp.exp(gc8[:, ch:ch + 1]))
                rhss.append(jnp.concatenate([vb, kbe], axis=1))
            for row, ch, rhs, corr in zip(rows, chans, rhss, _dot_each(tm1s, rhss)):
                sol = rhs + corr
                u_s[ch, pl.ds(row, c), :] = sol[:, :hd]
                w_s[ch, pl.ds(row, c), :] = sol[:, hd:].astype(BF16)
            return carry
        lax.fori_loop(0, n // (c * cpi), body, 0)

    prep(gtc_ref, tc, 0, GDN_PREP_CHUNKS)
    prep(gtl_ref, tl, tc, GDN_PREP_CHUNKS)

    st_s[...] = jnp.zeros_like(st_s)

    def scan(n, base):
        def body(i, carry):
            chs = range(4)
            idx = [i if ch < 2 else n // c - 1 - i for ch in chs]
            rows = [pl.multiple_of(base + idx[ch] * c, c) for ch in chs]
            s = [st_s[ch] for ch in chs]
            sb = [x.astype(BF16) for x in s]
            ws = [_dot(w_s[ch, pl.ds(rows[ch], c), :], sb[ch]) for ch in chs]
            qs = [_dot(qg_s[ch, pl.ds(rows[ch], c), :], sb[ch]) for ch in chs]
            vnb = [(u_s[ch, pl.ds(rows[ch], c), :] - ws[ch]).astype(BF16) for ch in chs]
            av = [_dot(a_s[ch, pl.ds(rows[ch], c), :], vnb[ch]) for ch in chs]
            kv = [_dot_tn(kd_s[ch, pl.ds(rows[ch], c), :], vnb[ch]) for ch in chs]
            for ch in chs:
                o_s[ch, pl.ds(rows[ch], c), :] = qs[ch] + av[ch]
                st_s[ch] = s[ch] * egl_s[ch, pl.ds(base // c + idx[ch], 1), :] + kv[ch]
            return carry
        lax.fori_loop(0, n // c, body, 0)

    scan(tc, 0)
    scan(tl, tc)

    nw = nw_ref[...]

    def finish(zg_ref, a_ref, n, base):
        def body(i, carry):
            r0 = pl.multiple_of(i * rt, rt)
            for r in range(2):
                y = o_s[r, pl.ds(base + r0, rt), :] + o_s[2 + r, pl.ds(base + r0, rt), :]
                yn = y * lax.rsqrt(jnp.mean(y * y, axis=-1, keepdims=True) + NORM_EPS) * nw
                g = zg_ref[pl.ds(r0, rt), r * hd:(r + 1) * hd].astype(F32)
                a_ref[pl.ds(r0, rt), r * hd:(r + 1) * hd] = (yn * _silu(g)).astype(BF16)
            return carry
        lax.fori_loop(0, n // rt, body, 0)

    finish(zgc_ref, ac_ref, tc, 0)
    finish(zgl_ref, al_ref, tl, tc)


def gdn_mixer(zc, zl, gtc, gtl, batch, conv_w, norm_w):
    tc = zc.shape[0] // batch
    tl = zl.shape[0] // batch
    hd = GDN_HEAD_DIM
    nqk = zc.shape[1] // hd // 6
    ttot = tc + tl
    kern = functools.partial(_gdn_kernel, tc=tc, tl=tl)

    def seg_specs(t):
        return [pl.BlockSpec((t, hd), lambda b, j: (b, j)),
                pl.BlockSpec((t, hd), lambda b, j: (b, nqk + j)),
                pl.BlockSpec((t, 2 * hd), lambda b, j: (b, nqk + j)),
                pl.BlockSpec((t, 2 * hd), lambda b, j: (b, 2 * nqk + j))]

    kw = conv_w.shape[0]
    ac, al = pl.pallas_call(
        kern,
        grid=(batch, nqk),
        in_specs=[pl.BlockSpec((1, 8, tc), lambda b, j: (b, j, 0)),
                  pl.BlockSpec((1, 8, tl), lambda b, j: (b, j, 0))]
                 + seg_specs(tc) + seg_specs(tl)
                 + [pl.BlockSpec((kw, hd), lambda b, j: (0, j)),
                    pl.BlockSpec((kw, hd), lambda b, j: (0, nqk + j)),
                    pl.BlockSpec((kw, 2 * hd), lambda b, j: (0, nqk + j)),
                    pl.BlockSpec((1, hd), lambda b, j: (0, 0))],
        out_specs=[pl.BlockSpec((tc, 2 * hd), lambda b, j: (b, j)),
                   pl.BlockSpec((tl, 2 * hd), lambda b, j: (b, j))],
        out_shape=[jax.ShapeDtypeStruct((batch * tc, 2 * nqk * hd), BF16),
                   jax.ShapeDtypeStruct((batch * tl, 2 * nqk * hd), BF16)],
        scratch_shapes=[pltpu.VMEM((max(tc, tl) + 16, 2 * hd), F32),
                        pltpu.VMEM((ttot, hd), F32), pltpu.VMEM((ttot, hd), F32),
                        pltpu.VMEM((ttot, 2 * hd), F32),
                        pltpu.VMEM((ttot, 128), F32),
                        pltpu.VMEM((4, ttot, hd), F32),
                        pltpu.VMEM((4, ttot, hd), BF16),
                        pltpu.VMEM((4, ttot, hd), BF16),
                        pltpu.VMEM((4, ttot, hd), BF16),
                        pltpu.VMEM((4, ttot, GDN_CHUNK), BF16),
                        pltpu.VMEM((4, ttot // GDN_CHUNK, hd), F32),
                        pltpu.VMEM((4, ttot, hd), F32),
                        pltpu.VMEM((4, hd, hd), F32)],
        compiler_params=_cparams(("parallel", "parallel")),
        name="gdn_mixer",
    )(gtc, gtl, zc, zc, zc, zc, zl, zl, zl, zl, conv_w, conv_w, conv_w, norm_w.reshape(1, hd))
    return al, ac


def gdn_gate_weights(w_ab, a_log, dt_bias):
    nv = a_log.shape[-1]
    nqk = nv // 2
    cols, alog, dtb = [], [], []
    for j in range(nqk):
        for kind in range(2):
            for d in range(2):
                for r in range(2):
                    hv = 2 * j + r
                    cols.append(w_ab[d][:, kind * nv + hv])
                    alog.append(a_log[d, hv] if kind == 0 else jnp.zeros((), F32))
                    dtb.append(dt_bias[d, hv] if kind == 0 else jnp.zeros((), F32))
    return jnp.stack(cols, axis=1), jnp.stack(alog)[:, None], jnp.stack(dtb)[:, None]


def _na_cases(rows):
    kh = min(WIN_H, rows)
    nbk = min(NA_KROWS, rows)
    cases, case_of_block, kstart = [], [], []
    for i in range(rows // NA_QROWS):
        r0 = i * NA_QROWS
        b0 = int(np.clip(r0 - kh // 2, 0, rows - nbk))
        sig = []
        for qr in range(NA_QROWS):
            q_row = r0 + qr
            row_start = int(np.clip(q_row - kh // 2, 0, rows - kh))
            for kr in range(nbk):
                k_row = b0 + kr
                ok = row_start <= k_row < row_start + kh
                dr = int(np.clip(k_row - q_row + WIN_H - 1, 0, 2 * WIN_H - 2))
                sig.append((ok, dr))
        sig = tuple(sig)
        if sig not in cases:
            cases.append(sig)
        case_of_block.append(cases.index(sig))
        kstart.append(b0 * GRID_W)
    return cases, case_of_block, kstart, nbk


def _na_bias_kernel(rpb_ref, o_ref, *, cases, nbk):
    h = pl.program_id(0)
    ndr, ndc = 2 * WIN_H - 1, 2 * WIN_W - 1
    kwid = min(WIN_W, GRID_W)
    qc = lax.broadcasted_iota(jnp.int32, (GRID_W, 128), 0)
    lane = lax.broadcasted_iota(jnp.int32, (GRID_W, 128), 1)
    kc = lane % GRID_W
    second = (lane // GRID_W).astype(F32)
    col_start = jnp.clip(qc - kwid // 2, 0, GRID_W - kwid)
    col_ok = (kc >= col_start) & (kc < col_start + kwid)
    dc = jnp.clip(kc - qc + WIN_W - 1, 0, ndc - 1)
    base = h * (ndr * ndc)
    cache = {}

    def tile_for(dr0, dr1):
        key = (dr0, dr1)
        if key not in cache:
            val = jnp.zeros((GRID_W, 128), F32)
            for n in range(ndc):
                s0 = rpb_ref[base + dr0 * ndc + n]
                s1 = rpb_ref[base + dr1 * ndc + n]
                val = jnp.where(dc == n, s0 + (s1 - s0) * second, val)
            cache[key] = val
        return cache[key]

    for ci, sig in enumerate(cases):
        for qr in range(NA_QROWS):
            for lt in range(nbk // 2):
                ok0, dr0 = sig[qr * nbk + 2 * lt]
                ok1, dr1 = sig[qr * nbk + 2 * lt + 1]
                if not (ok0 or ok1):
                    tile = jnp.full((GRID_W, 128), NEG_BIG, F32)
                else:
                    row_pen = (float(ok0) - 1.0) + (float(ok1) - float(ok0)) * second
                    tile = jnp.where(col_ok, tile_for(dr0, dr1), NEG_BIG) - row_pen * NEG_BIG
                o_ref[0, ci, qr * GRID_W:(qr + 1) * GRID_W, lt * 128:(lt + 1) * 128] = tile


def na_bias_tables(rpb, cases, nbk):
    nh = rpb.shape[0]
    ncase = len(cases)
    qblk = NA_QROWS * GRID_W
    return pl.pallas_call(
        functools.partial(_na_bias_kernel, cases=cases, nbk=nbk),
        grid=(nh,),
        in_specs=[pl.BlockSpec(memory_space=pltpu.SMEM)],
        out_specs=pl.BlockSpec((1, ncase, qblk, nbk * GRID_W), lambda h: (h, 0, 0, 0)),
        out_shape=jax.ShapeDtypeStruct((nh, ncase, qblk, nbk * GRID_W), F32),
        compiler_params=_cparams(("arbitrary",)),
        name="na_bias_tables",
    )(rpb.reshape(-1))


def _na_kernel(case_ref, kstart_ref, q_ref, k_ref, v_ref, qc_ref, kc_ref, vc_ref, bias_ref,
               ol_ref, oc_ref, *, nloc, scale, group):
    i = pl.program_id(2)
    kc = kc_ref[...]
    vc = vc_ref[...]

    @pl.when(i == 0)
    def _():
        s = _dot_nt(qc_ref[...], kc) * scale
        p = jnp.exp(s - jnp.max(s, axis=-1, keepdims=True))
        o = _dot(p.astype(BF16), vc) / jnp.sum(p, axis=-1, keepdims=True)
        oc_ref[...] = o.astype(BF16)

    gs = range(group)
    qblk = q_ref.shape[0] // group
    q = [q_ref[g * qblk:(g + 1) * qblk, :] for g in gs]
    ks = [pl.multiple_of(kstart_ref[i * group + g], GRID_W) for g in gs]
    s_loc = [_dot_nt(q[g], k_ref[pl.ds(ks[g], nloc), :]) * scale + bias_ref[0, case_ref[i * group + g]]
             for g in gs]
    s_ctx = [_dot_nt(q[g], kc) * scale for g in gs]
    m = [jnp.maximum(jnp.max(s_loc[g], axis=-1, keepdims=True), jnp.max(s_ctx[g], axis=-1, keepdims=True))
         for g in gs]
    p_loc = [jnp.exp(s_loc[g] - m[g]) for g in gs]
    p_ctx = [jnp.exp(s_ctx[g] - m[g]) for g in gs]
    denom = [jnp.sum(p_loc[g], axis=-1, keepdims=True) + jnp.sum(p_ctx[g], axis=-1, keepdims=True)
             for g in gs]
    pv = [_dot(p_loc[g].astype(BF16), v_ref[pl.ds(ks[g], nloc), :]) + _dot(p_ctx[g].astype(BF16), vc)
          for g in gs]
    for g in gs:
        ol_ref[g * qblk:(g + 1) * qblk, :] = (pv[g] / denom[g]).astype(BF16)


def na_mixer(zc, zl, batch, rpb):
    tc = zc.shape[0] // batch
    tl = zl.shape[0] // batch
    d = zc.shape[1] // 3
    nh = NA_HEADS
    hd = d // nh
    rows = tl // GRID_W
    cases, case_of_block, kstart, nbk = _na_cases(rows)
    bias = na_bias_tables(rpb, cases, nbk)
    qblk = NA_QROWS * GRID_W
    nblk = rows // NA_QROWS
    group = NA_GROUP if nblk % NA_GROUP == 0 else 1
    nstep = nblk // group
    nloc = nbk * GRID_W
    grid_spec = pltpu.PrefetchScalarGridSpec(
        num_scalar_prefetch=2,
        grid=(nh, batch, nstep),
        in_specs=[pl.BlockSpec((group * qblk, hd), lambda h, b, i, *_: (b * nstep + i, h)),
                  pl.BlockSpec((tl, hd), lambda h, b, i, *_: (b, nh + h)),
                  pl.BlockSpec((tl, hd), lambda h, b, i, *_: (b, 2 * nh + h)),
                  pl.BlockSpec((tc, hd), lambda h, b, i, *_: (b, h)),
                  pl.BlockSpec((tc, hd), lambda h, b, i, *_: (b, nh + h)),
                  pl.BlockSpec((tc, hd), lambda h, b, i, *_: (b, 2 * nh + h)),
                  pl.BlockSpec((1, len(cases), qblk, nloc), lambda h, b, i, *_: (h, 0, 0, 0))],
        out_specs=[pl.BlockSpec((group * qblk, hd), lambda h, b, i, *_: (b * nstep + i, h)),
                   pl.BlockSpec((tc, hd), lambda h, b, i, *_: (b, h))])
    ol, oc = pl.pallas_call(
        functools.partial(_na_kernel, nloc=nloc, scale=hd ** -0.5, group=group),
        grid_spec=grid_spec,
        out_shape=[jax.ShapeDtypeStruct((batch * tl, d), BF16),
                   jax.ShapeDtypeStruct((batch * tc, d), BF16)],
        compiler_params=_cparams(("parallel", "parallel", "arbitrary")),
        name="na_mixer",
    )(jnp.asarray(case_of_block, jnp.int32), jnp.asarray(kstart, jnp.int32),
      zl, zl, zl, zc, zc, zc, bias)
    return ol, oc


TM = 512
TN_PROJ = 1024
TK_OUT = 1024
TF_MLP = 1024


def kernel(x, c, ctx, c_ctx, ada_w, ada_b, norm_g, mlp_w1, mlp_w2, ret_w_in, ret_w_out, gdn_w_in, gdn_conv_w, gdn_w_ab, gdn_a_log, gdn_dt_bias, gdn_norm_w, gdn_w_out, na_w_in, na_w_out, na_rpb):
    batch, t, d = x.shape
    tctx = ctx.shape[1]
    depth = ada_w.shape[0]

    nrow = -(-(batch + 1) // 16) * 16
    cvec = jnp.zeros((nrow, d), F32).at[0].set(c_ctx).at[1:batch + 1].set(c)
    mods = ada_modulation(cvec, ada_w, ada_b).reshape(depth, nrow, N_MOD, d)

    xl = x.reshape(batch * t, d)
    xc = ctx.reshape(batch * tctx, d)
    ret_lg, ret_cos, ret_sin = retention_tables(t, ret_w_in.shape[2] // 6 // RET_HEADS)

    i_ret = i_gdn = i_na = 0
    for i in range(depth):
        last = i == depth - 1
        m_ctx = mods[i, 0:1]
        m_lat = mods[i, 1:batch + 1]
        g = norm_g[i]
        kind = i % 3

        def project(w, out_dtype=BF16, tn=TN_PROJ):
            zc = norm_mod_matmul(xc, m_ctx, batch * tctx, g[0], w, 0, 1, out_dtype, TM, tn)
            zl = norm_mod_matmul(xl, m_lat, t, g[0], w, 0, 1, out_dtype, TM, tn)
            return zc, zl

        if kind == 0:
            zc, zl = project(ret_w_in[i_ret].astype(BF16))
            al, ac = retention_mixer(zc, zl, batch, ret_lg, ret_cos, ret_sin, not last)
            w_out = ret_w_out[i_ret].astype(BF16)
            i_ret += 1
        elif kind == 1:
            zc, zl = project(gdn_w_in[i_gdn].astype(BF16))
            w_gate, alog_col, dtb_col = gdn_gate_weights(gdn_w_ab[i_gdn], gdn_a_log[i_gdn],
                                                        gdn_dt_bias[i_gdn])
            abc, abl = project(w_gate.astype(BF16), out_dtype=F32)
            gtc = gdn_gates(abc.reshape(batch, tctx, -1).transpose(0, 2, 1), alog_col, dtb_col)
            gtl = gdn_gates(abl.reshape(batch, t, -1).transpose(0, 2, 1), alog_col, dtb_col)
            al, ac = gdn_mixer(zc, zl, gtc, gtl, batch, gdn_conv_w[i_gdn], gdn_norm_w[i_gdn])
            w_out = gdn_w_out[i_gdn].astype(BF16)
            i_gdn += 1
        else:
            zc, zl = project(na_w_in[i_na].astype(BF16))
            al, ac = na_mixer(zc, zl, batch, na_rpb[i_na])
            w_out = na_w_out[i_na].astype(BF16)
            i_na += 1

        w1 = mlp_w1[i].astype(BF16)
        w2 = mlp_w2[i].astype(BF16)
        xl = out_proj_residual(al, w_out, xl, m_lat, t, g[1], 2, TM, TK_OUT)
        xl = mlp_residual(xl, m_lat, t, g[2], g[3], w1, w2, TM, TF_MLP)
        if not last:
            xc = out_proj_residual(ac, w_out, xc, m_ctx, batch * tctx, g[1], 2, TM, TK_OUT)
            xc = mlp_residual(xc, m_ctx, batch * tctx, g[2], g[3], w1, w2, TM, TF_MLP)
    return xl.reshape(batch, t, d)
```

```python
import functools
import math

import numpy as np
import jax
import jax.numpy as jnp
from jax import lax
from jax.experimental import pallas as pl
from jax.experimental.pallas import tpu as pltpu

F32 = jnp.float32
BF16 = jnp.bfloat16

NORM_EPS = 1e-6
L2_EPS = 1e-6
ROPE_BASE = 10000.0
N_MOD = 6
GRID_W = 64

RET_HEADS = 8
RET_CHUNK = 256
GDN_HEAD_DIM = 128
GDN_CHUNK = 128
GDN_CONV = 5
NA_HEADS = 16
WIN_H = 8
WIN_W = 16
NA_QROWS = 2
NA_KROWS = 10
NEG_BIG = -1e30
NA_GROUP = 4

V7X_VMEM_LIMIT_BYTES = 56 * 1024 * 1024


def _cparams(sem):
    return pltpu.CompilerParams(dimension_semantics=sem, vmem_limit_bytes=V7X_VMEM_LIMIT_BYTES)


def _sigmoid(x):
    return 1.0 / (1.0 + jnp.exp(-x))


def _silu(x):
    return x * _sigmoid(x)


def _dot(a, b):
    return jnp.dot(a, b, preferred_element_type=F32)


def _dot_nt(a, b):
    return lax.dot_general(a, b, (((1,), (1,)), ((), ())), preferred_element_type=F32)


def _dot_tn(a, b):
    return lax.dot_general(a, b, (((0,), (0,)), ((), ())), preferred_element_type=F32)


def _rms(y, g):
    return y * lax.rsqrt(jnp.mean(y * y, axis=-1, keepdims=True) + NORM_EPS) * g


def _ada_kernel(c_ref, w_ref, b_ref, o_ref):
    s = _silu(c_ref[...]).astype(BF16)
    o_ref[0] = _dot(s, w_ref[0].astype(BF16)) + b_ref[0]


def ada_modulation(cvec, ada_w, ada_b, tn=1024):
    depth, d, n = ada_w.shape
    r = cvec.shape[0]
    return pl.pallas_call(
        _ada_kernel,
        grid=(depth, n // tn),
        in_specs=[pl.BlockSpec((r, d), lambda l, j: (0, 0)),
                  pl.BlockSpec((1, d, tn), lambda l, j: (l, 0, j)),
                  pl.BlockSpec((1, 1, tn), lambda l, j: (l, 0, j))],
        out_specs=pl.BlockSpec((1, r, tn), lambda l, j: (l, 0, j)),
        out_shape=jax.ShapeDtypeStruct((depth, r, n), F32),
        compiler_params=_cparams(("parallel", "parallel")),
        name="ada_modulation",
    )(cvec, ada_w, ada_b.reshape(depth, 1, n))


def _modulated_norm(x, g, m_ref, shift_idx, scale_idx):
    y = _rms(x, g)
    return y * (1.0 + m_ref[0, scale_idx:scale_idx + 1, :]) + m_ref[0, shift_idx:shift_idx + 1, :]


def _nmm_kernel(x_ref, m_ref, g_ref, w_ref, o_ref, h_ref, *, shift_idx, scale_idx):
    @pl.when(pl.program_id(1) == 0)
    def _():
        h_ref[...] = _modulated_norm(x_ref[...], g_ref[...], m_ref, shift_idx, scale_idx).astype(BF16)

    o_ref[...] = _dot(h_ref[...], w_ref[...]).astype(o_ref.dtype)


def norm_mod_matmul(x, mods, tokens_per_mod, g, w, shift_idx, scale_idx, out_dtype, tm, tn):
    n, d = x.shape
    nout = w.shape[1]
    tm = min(tm, tokens_per_mod)
    tn = min(tn, nout)
    bpm = tokens_per_mod // tm
    return pl.pallas_call(
        functools.partial(_nmm_kernel, shift_idx=shift_idx, scale_idx=scale_idx),
        grid=(n // tm, nout // tn),
        in_specs=[pl.BlockSpec((tm, d), lambda i, j: (i, 0)),
                  pl.BlockSpec((1, N_MOD, d), lambda i, j: (i // bpm, 0, 0)),
                  pl.BlockSpec((1, d), lambda i, j: (0, 0)),
                  pl.BlockSpec((d, tn), lambda i, j: (0, j))],
        out_specs=pl.BlockSpec((tm, tn), lambda i, j: (i, j)),
        out_shape=jax.ShapeDtypeStruct((n, nout), out_dtype),
        scratch_shapes=[pltpu.VMEM((tm, d), BF16)],
        compiler_params=_cparams(("parallel", "arbitrary")),
        name="norm_mod_matmul",
    )(x, mods, g.reshape(1, d), w)


def _rms_residual_slabs(y_ref, x_ref, gate, g_ref, o_ref):
    ns, _, tn = y_ref.shape
    ss = jnp.sum(y_ref[0] * y_ref[0], axis=-1, keepdims=True)
    for j in range(1, ns):
        ss = ss + jnp.sum(y_ref[j] * y_ref[j], axis=-1, keepdims=True)
    inv = lax.rsqrt(ss / (ns * tn) + NORM_EPS)
    for j in range(ns):
        sl = slice(j * tn, (j + 1) * tn)
        o_ref[:, sl] = x_ref[:, sl] + gate[:, sl] * (y_ref[j] * inv * g_ref[:, sl])


def _out_res_kernel(a_ref, w_ref, x_ref, m_ref, g_ref, o_ref, y_ref, *, gate_idx, nj):
    j = pl.program_id(1)
    y_ref[j] = _dot(a_ref[...], w_ref[...])

    @pl.when(j == nj - 1)
    def _():
        _rms_residual_slabs(y_ref, x_ref, m_ref[0, gate_idx:gate_idx + 1, :], g_ref, o_ref)


def out_proj_residual(a, w, x, mods, tokens_per_mod, g, gate_idx, tm, tn):
    n, kdim = a.shape
    d = w.shape[1]
    tm = min(tm, tokens_per_mod)
    bpm = tokens_per_mod // tm
    nj = d // tn
    return pl.pallas_call(
        functools.partial(_out_res_kernel, gate_idx=gate_idx, nj=nj),
        grid=(n // tm, nj),
        in_specs=[pl.BlockSpec((tm, kdim), lambda i, j: (i, 0)),
                  pl.BlockSpec((kdim, tn), lambda i, j: (0, j)),
                  pl.BlockSpec((tm, d), lambda i, j: (i, 0)),
                  pl.BlockSpec((1, N_MOD, d), lambda i, j: (i // bpm, 0, 0)),
                  pl.BlockSpec((1, d), lambda i, j: (0, 0))],
        out_specs=pl.BlockSpec((tm, d), lambda i, j: (i, 0)),
        out_shape=jax.ShapeDtypeStruct((n, d), F32),
        scratch_shapes=[pltpu.VMEM((nj, tm, tn), F32)],
        compiler_params=_cparams(("parallel", "arbitrary")),
        name="out_proj_residual",
    )(a, w, x, mods, g.reshape(1, d))


def _mlp_kernel(x_ref, m_ref, g2_ref, g3_ref, w1_ref, w2_ref, o_ref, h_ref, a_ref, y_ref, *, n1, n2, tf):
    k = pl.program_id(1)

    @pl.when(k == 0)
    def _():
        h_ref[...] = _modulated_norm(x_ref[...], g2_ref[...], m_ref, 3, 4).astype(BF16)

    @pl.when(k < n1)
    def _():
        a = jnp.maximum(_dot(h_ref[...], w1_ref[...]), 0.0)
        a_ref[jnp.minimum(k, n1 - 1)] = (a * a).astype(BF16)

    @pl.when(k >= n1)
    def _():
        acc = _dot(a_ref[0], w2_ref[0:tf, :])
        for s in range(1, n1):
            acc = acc + _dot(a_ref[s], w2_ref[s * tf:(s + 1) * tf, :])
        y_ref[jnp.maximum(k - n1, 0)] = acc

    @pl.when(k == n1 + n2 - 1)
    def _():
        _rms_residual_slabs(y_ref, x_ref, m_ref[0, 5:6, :], g3_ref, o_ref)


def mlp_residual(x, mods, tokens_per_mod, g2, g3, w1, w2, tm, tf, tn):
    n, d = x.shape
    dff = w1.shape[1]
    tm = min(tm, tokens_per_mod)
    bpm = tokens_per_mod // tm
    n1 = dff // tf
    n2 = d // tn
    return pl.pallas_call(
        functools.partial(_mlp_kernel, n1=n1, n2=n2, tf=tf),
        grid=(n // tm, n1 + n2),
        in_specs=[pl.BlockSpec((tm, d), lambda i, k: (i, 0)),
                  pl.BlockSpec((1, N_MOD, d), lambda i, k: (i // bpm, 0, 0)),
                  pl.BlockSpec((1, d), lambda i, k: (0, 0)),
                  pl.BlockSpec((1, d), lambda i, k: (0, 0)),
                  pl.BlockSpec((d, tf), lambda i, k: (0, jnp.minimum(k, n1 - 1))),
                  pl.BlockSpec((dff, tn), lambda i, k: (0, jnp.maximum(k - n1, 0)))],
        out_specs=pl.BlockSpec((tm, d), lambda i, k: (i, 0)),
        out_shape=jax.ShapeDtypeStruct((n, d), F32),
        scratch_shapes=[pltpu.VMEM((tm, d), BF16), pltpu.VMEM((n1, tm, tf), BF16),
                        pltpu.VMEM((n2, tm, tn), F32)],
        compiler_params=_cparams(("parallel", "arbitrary")),
        name="mlp_residual",
    )(x, mods, g2.reshape(1, d), g3.reshape(1, d), w1, w2)


def _ret_kernel(lg_ref, cos_ref, sin_ref, qc_ref, kc_ref, vc_ref, gc_ref, ql_ref, kl_ref, vl_ref, gl_ref,
                *rest, tc, tl, dk, dv, with_ctx_out):
    if with_ctx_out:
        ac_ref, al_ref, qs_ref, ks_ref, of_ref, ob_ref, rf_ref, rb_ref = rest
    else:
        al_ref, qs_ref, ks_ref, of_ref, ob_ref, rf_ref, rb_ref = rest
        ac_ref = None
    c = RET_CHUNK
    half = dk // 2
    k_scale = dk ** -0.5

    qs_ref[0:tc, :] = qc_ref[...]
    ks_ref[0:tc, :] = (kc_ref[...].astype(F32) * k_scale).astype(BF16)

    rt = 256

    def rope_body(i, carry):
        r0 = pl.multiple_of(i * rt, rt)
        cs = cos_ref[pl.ds(r0, rt), :]
        sn = sin_ref[pl.ds(r0, rt), :]
        for src, dst, scale in ((ql_ref, qs_ref, 1.0), (kl_ref, ks_ref, k_scale)):
            t = src[pl.ds(r0, rt), :].astype(F32) * scale
            t1, t2 = t[:, :half], t[:, half:]
            dst[pl.ds(tc + r0, rt), :] = jnp.concatenate(
                [t1 * cs - t2 * sn, t1 * sn + t2 * cs], axis=-1).astype(BF16)
        return carry

    lax.fori_loop(0, tl // rt, rope_body, 0)

    lgf = lg_ref[0, 0:1, 0:1]
    lgb = lg_ref[0, 1:2, 0:1]
    ci = lax.broadcasted_iota(jnp.int32, (c, c), 0)
    mi = lax.broadcasted_iota(jnp.int32, (c, c), 1)
    dist = (ci - mi).astype(F32)
    intra_f = jnp.where(dist >= 0, jnp.exp(lgf * jnp.maximum(dist, 0.0)), 0.0)
    intra_b = jnp.where(dist <= 0, jnp.exp(lgb * jnp.maximum(-dist, 0.0)), 0.0)
    pos = lax.broadcasted_iota(jnp.int32, (c, 1), 0).astype(F32)
    xi_f = jnp.exp(lgf * (pos + 1.0))
    zeta_f = jnp.exp(lgf * (c - 1.0 - pos))
    xi_b = jnp.exp(lgb * (c - pos))
    zeta_b = jnp.exp(lgb * pos)
    cd_f = jnp.exp(lgf * float(c))
    cd_b = jnp.exp(lgb * float(c))

    rf_ref[...] = jnp.zeros_like(rf_ref)
    rb_ref[...] = jnp.zeros_like(rb_ref)
    dirs = ((rf_ref, of_ref, intra_f, xi_f, zeta_f, cd_f), (rb_ref, ob_ref, intra_b, xi_b, zeta_b, cd_b))

    def make_body(v_ref, base, n):
        def body(i, carry):
            loc = (pl.multiple_of(i * c, c), pl.multiple_of((n - 1 - i) * c, c))
            qt = [qs_ref[pl.ds(base + l, c), :] for l in loc]
            kt = [ks_ref[pl.ds(base + l, c), :] for l in loc]
            vt = [v_ref[pl.ds(l, c), :] for l in loc]
            r = [dr[0][...] for dr in dirs]
            s = [_dot_nt(q, k) for q, k in zip(qt, kt)]
            qr = [_dot(q, x.astype(BF16)) for q, x in zip(qt, r)]
            kv = [_dot_tn(k, (v.astype(F32) * dr[4]).astype(BF16)) for k, v, dr in zip(kt, vt, dirs)]
            sv = [_dot((x * dr[2]).astype(BF16), v) for x, v, dr in zip(s, vt, dirs)]
            for d, dr in enumerate(dirs):
                dr[1][pl.ds(base + loc[d], c), :] = sv[d] + qr[d] * dr[3]
                dr[0][...] = r[d] * dr[5] + kv[d]
            return carry
        return body

    lax.fori_loop(0, tc // c, make_body(vc_ref, 0, tc // c), 0)
    lax.fori_loop(0, tl // c, make_body(vl_ref, tc, tl // c), 0)

    def finish(g_ref, a_ref, base, n):
        def body(i, carry):
            r0 = pl.multiple_of(i * rt, rt)
            y = of_ref[pl.ds(base + r0, rt), :] + ob_ref[pl.ds(base + r0, rt), :]
            yn = y * lax.rsqrt(jnp.mean(y * y, axis=-1, keepdims=True) + NORM_EPS)
            g = g_ref[pl.ds(r0, rt), :].astype(F32)
            a_ref[pl.ds(r0, rt), :] = (_silu(g) * yn).astype(BF16)
            return carry
        lax.fori_loop(0, n // rt, body, 0)

    if with_ctx_out:
        finish(gc_ref, ac_ref, 0, tc)
    finish(gl_ref, al_ref, tc, tl)


def retention_mixer(zc, zl, batch, lg, cos, sin, with_ctx_out):
    tc = zc.shape[0] // batch
    tl = zl.shape[0] // batch
    width = zc.shape[1]
    dk = width // 6 // RET_HEADS
    dv = 2 * dk
    h = RET_HEADS
    kern = functools.partial(_ret_kernel, tc=tc, tl=tl, dk=dk, dv=dv, with_ctx_out=with_ctx_out)

    def seg_specs(t):
        return [pl.BlockSpec((t, dk), lambda b, j: (b, j)),
                pl.BlockSpec((t, dk), lambda b, j: (b, h + j)),
                pl.BlockSpec((t, dv), lambda b, j: (b, h + j)),
                pl.BlockSpec((t, dv), lambda b, j: (b, 2 * h + j))]

    out_shape = [jax.ShapeDtypeStruct((batch * tl, h * dv), BF16)]
    out_specs = [pl.BlockSpec((tl, dv), lambda b, j: (b, j))]
    if with_ctx_out:
        out_shape = [jax.ShapeDtypeStruct((batch * tc, h * dv), BF16)] + out_shape
        out_specs = [pl.BlockSpec((tc, dv), lambda b, j: (b, j))] + out_specs
    outs = pl.pallas_call(
        kern,
        grid=(batch, h),
        in_specs=[pl.BlockSpec((1, 8, 128), lambda b, j: (j, 0, 0)),
                  pl.BlockSpec((tl, dk // 2), lambda b, j: (0, 0)),
                  pl.BlockSpec((tl, dk // 2), lambda b, j: (0, 0))] + seg_specs(tc) + seg_specs(tl),
        out_specs=out_specs,
        out_shape=out_shape,
        scratch_shapes=[pltpu.VMEM((tc + tl, dk), BF16), pltpu.VMEM((tc + tl, dk), BF16),
                        pltpu.VMEM((tc + tl, dv), F32), pltpu.VMEM((tc + tl, dv), F32),
                        pltpu.VMEM((dk, dv), F32), pltpu.VMEM((dk, dv), F32)],
        compiler_params=_cparams(("parallel", "parallel")),
        name="retention_mixer",
    )(lg, cos, sin, zc, zc, zc, zc, zl, zl, zl, zl)
    if with_ctx_out:
        return outs[1], outs[0]
    return outs[0], None


def retention_tables(t, dk):
    fwd = jnp.log(1.0 - 2.0 ** (-5.0 - jnp.arange(RET_HEADS, dtype=F32)))
    lg = jnp.zeros((RET_HEADS, 8, 128), F32)
    lg = lg.at[:, 0, :].set(fwd[:, None]).at[:, 1, :].set(fwd[::-1][:, None])
    tt = jnp.arange(t)
    row = (tt // GRID_W).astype(F32)
    col = (tt % GRID_W).astype(F32)
    n_pairs = dk // 2
    inv = ROPE_BASE ** (-jnp.arange(0, n_pairs, 2, dtype=F32) / n_pairs)
    ang = jnp.concatenate([row[:, None] * inv, col[:, None] * inv], axis=-1)
    return lg, jnp.cos(ang), jnp.sin(ang)


def _split3(x):
    hi = x.astype(BF16)
    r1 = x - hi.astype(F32)
    mid = r1.astype(BF16)
    lo = (r1 - mid.astype(F32)).astype(BF16)
    return hi, mid, lo


def _gdn_gates_kernel(ab_ref, alog_ref, dtb_ref, o_ref, *, t):
    c = GDN_CHUNK
    nrow = ab_ref.shape[1]
    rowid = lax.broadcasted_iota(jnp.int32, (nrow, 1), 0) % 8
    mi = lax.broadcasted_iota(jnp.int32, (c, c), 0)
    ci = lax.broadcasted_iota(jnp.int32, (c, c), 1)
    pre = (mi <= ci).astype(BF16)
    suf = (mi >= ci).astype(BF16)
    a_neg = -jnp.exp(alog_ref[...])
    dtb = dtb_ref[...]
    for w in range(t // c):
        x = ab_ref[0, :, w * c:(w + 1) * c]
        z = x + dtb
        g = a_neg * (jnp.maximum(z, 0.0) + jnp.log1p(jnp.exp(-jnp.abs(z))))
        parts = _split3(g)
        cp = sum(_dot(p, pre) for p in parts)
        cs = sum(_dot(p, suf) for p in parts)
        o_ref[0, :, w * c:(w + 1) * c] = jnp.where(rowid < 2, cp, jnp.where(rowid < 4, cs, _sigmoid(x)))


def gdn_gates(ab_t, alog_col, dtb_col):
    b, nrow, t = ab_t.shape
    return pl.pallas_call(
        functools.partial(_gdn_gates_kernel, t=t),
        grid=(b,),
        in_specs=[pl.BlockSpec((1, nrow, t), lambda i: (i, 0, 0)),
                  pl.BlockSpec((nrow, 1), lambda i: (0, 0)),
                  pl.BlockSpec((nrow, 1), lambda i: (0, 0))],
        out_specs=pl.BlockSpec((1, nrow, t), lambda i: (i, 0, 0)),
        out_shape=jax.ShapeDtypeStruct((b, nrow, t), F32),
        compiler_params=_cparams(("parallel",)),
        name="gdn_gates",
    )(ab_t, alog_col, dtb_col)


def _dot_each(a_list, b_list):
    return [_dot(a.astype(BF16), b.astype(BF16)) for a, b in zip(a_list, b_list)]


GDN_INV_BASE = 16
GDN_PREP_CHUNKS = 2


def _unit_triangular_inverse_minus_eye(ms, ri, li):
    n = ms[0].shape[0]
    s = GDN_INV_BASE
    inblk = (ri // s) == (li // s)
    ps = [jnp.where(inblk, -m, 0.0) for m in ms]
    qs = ps
    ps = _dot_each(ps, ps)
    level = 2
    while 2 * level < s:
        both = _dot_each([jnp.concatenate([q, p], axis=0) for q, p in zip(qs, ps)], ps)
        qs = [q + p + b[:n] for q, p, b in zip(qs, ps, both)]
        ps = [b[n:] for b in both]
        level *= 2
    qp = _dot_each(qs, ps)
    eye = jnp.where(ri == li, 1.0, 0.0)
    xs = [eye + q + p + t for q, p, t in zip(qs, ps, qp)]
    while s < n:
        off = ((ri // (2 * s)) == (li // (2 * s))) & ((ri // s) != (li // s))
        cxs = _dot_each([jnp.where(off, m, 0.0) for m in ms], xs)
        xcx = _dot_each(xs, cxs)
        xs = [x - t for x, t in zip(xs, xcx)]
        s *= 2
    return [x - eye for x in xs]


def _gdn_kernel(gtc_ref, gtl_ref, zqc_ref, zkc_ref, zvc_ref, zgc_ref, zql_ref, zkl_ref, zvl_ref, zgl_ref,
                cwq_ref, cwk_ref, cwv_ref, nw_ref, ac_ref, al_ref,
                pad_ref, q_s, k_s, v_s, gcol_s, u_s, w_s, qg_s, kd_s, a_s, egl_s, o_s, st_s, *, tc, tl):
    c = GDN_CHUNK
    hd = GDN_HEAD_DIM
    ttot = tc + tl
    rt = 256
    halo = 8
    kw = GDN_CONV
    lead = halo - (kw - 1) // 2

    def conv_seg(z_ref, cw_ref, n, width, dst_ref, base, l2_scale):
        zero = jnp.zeros((halo, width), F32)
        pad_ref[0:halo, 0:width] = zero
        pad_ref[halo + n:2 * halo + n, 0:width] = zero

        def fill(i, carry):
            r0 = pl.multiple_of(i * rt, rt)
            pad_ref[pl.ds(halo + r0, rt), 0:width] = z_ref[pl.ds(r0, rt), :].astype(F32)
            return carry
        lax.fori_loop(0, n // rt, fill, 0)
        cw = cw_ref[...]

        def body(i, carry):
            r0 = pl.multiple_of(i * rt, rt)
            xx = pad_ref[pl.ds(r0, rt + 2 * halo), 0:width]
            acc = xx[lead:lead + rt, :] * cw[0:1, :]
            for k in range(1, kw):
                acc = acc + xx[lead + k:lead + k + rt, :] * cw[k:k + 1, :]
            y = _silu(acc)
            if l2_scale is not None:
                for j in range(width // hd):
                    yj = y[:, j * hd:(j + 1) * hd]
                    yj = yj * lax.rsqrt(jnp.sum(yj * yj, axis=-1, keepdims=True) + L2_EPS) * l2_scale
                    dst_ref[pl.ds(base + r0, rt), j * hd:(j + 1) * hd] = yj
            else:
                dst_ref[pl.ds(base + r0, rt), :] = y
            return carry
        lax.fori_loop(0, n // rt, body, 0)

    for z_ref, n, base in ((zqc_ref, tc, 0), (zql_ref, tl, tc)):
        conv_seg(z_ref, cwq_ref, n, hd, q_s, base, hd ** -0.5)
    for z_ref, n, base in ((zkc_ref, tc, 0), (zkl_ref, tl, tc)):
        conv_seg(z_ref, cwk_ref, n, hd, k_s, base, 1.0)
    for z_ref, n, base in ((zvc_ref, tc, 0), (zvl_ref, tl, tc)):
        conv_seg(z_ref, cwv_ref, n, 2 * hd, v_s, base, None)

    def gate_cols(gt_ref, n, base):
        def body(i, carry):
            r0 = pl.multiple_of(i * c, c)
            tile = gt_ref[0, :, pl.ds(r0, c)]
            gcol_s[pl.ds(base + r0, c), :] = jnp.concatenate([tile] * (c // 8), axis=0).T
            return carry
        lax.fori_loop(0, n // c, body, 0)

    gate_cols(gtc_ref, tc, 0)
    gate_cols(gtl_ref, tl, tc)

    ri = lax.broadcasted_iota(jnp.int32, (c, c), 0)
    li = lax.broadcasted_iota(jnp.int32, (c, c), 1)

    def prep(gt_ref, n, base, cpi):
        def body(i, carry):
            rows, chans, mmats, rhss = [], [], [], []
            for cc in range(cpi):
                r0 = pl.multiple_of((i * cpi + cc) * c, c)
                row = base + r0
                g8 = gt_ref[0, :, pl.ds(r0, c)]
                gc8 = gcol_s[pl.ds(row, c), :]
                kb = k_s[pl.ds(row, c), :]
                qb = q_s[pl.ds(row, c), :]
                kbb = kb.astype(BF16)
                kk = _dot_nt(kbb, kbb)
                qk = _dot_nt(qb.astype(BF16), kbb)
                for d in range(2):
                    incl = (ri >= li) if d == 0 else (ri <= li)
                    strict = (ri > li) if d == 0 else (ri < li)
                    for r in range(2):
                        ch = 2 * d + r
                        g_row = g8[ch:ch + 1, :]
                        g_b = jnp.broadcast_to(gc8[:, ch:ch + 1], (c, hd))
                        beta_b = jnp.broadcast_to(gc8[:, 4 + ch:5 + ch], (c, hd))
                        eg_b = jnp.exp(g_b)
                        decay = jnp.exp(jnp.where(incl, g_b - g_row, -jnp.inf))
                        mmats.append(jnp.where(strict, kk * decay, 0.0) * beta_b)
                        rows.append(row)
                        chans.append(ch)
                        qg_s[ch, pl.ds(row, c), :] = (qb * eg_b).astype(BF16)
                        a_s[ch, pl.ds(row, c), :] = (qk * decay).astype(BF16)
                        g_last = g_row[:, c - 1:c] if d == 0 else g_row[:, 0:1]
                        kd_s[ch, pl.ds(row, c), :] = (kb * jnp.exp(g_last - g_b)).astype(BF16)
                        egl_s[ch, pl.ds(base // c + i * cpi + cc, 1), :] = jnp.broadcast_to(
                            jnp.exp(g_last), (1, hd))
                        vb = v_s[pl.ds(row, c), r * hd:(r + 1) * hd] * beta_b
                        rhss.append(jnp.concatenate([vb, kb * (beta_b * eg_b)], axis=1))
            tm1s = _unit_triangular_inverse_minus_eye(mmats, ri, li)
            for row, ch, rhs, corr in zip(rows, chans, rhss, _dot_each(tm1s, rhss)):
                sol = rhs + corr
                u_s[ch, pl.ds(row, c), :] = sol[:, :hd]
                w_s[ch, pl.ds(row, c), :] = sol[:, hd:].astype(BF16)
            return carry
        lax.fori_loop(0, n // (c * cpi), body, 0)

    prep(gtc_ref, tc, 0, GDN_PREP_CHUNKS)
    prep(gtl_ref, tl, tc, GDN_PREP_CHUNKS)

    st_s[...] = jnp.zeros_like(st_s)

    def scan(n, base):
        def body(i, carry):
            chs = range(4)
            idx = [i if ch < 2 else n // c - 1 - i for ch in chs]
            rows = [pl.multiple_of(base + idx[ch] * c, c) for ch in chs]
            s = [st_s[ch] for ch in chs]
            sb = [x.astype(BF16) for x in s]
            ws = [_dot(w_s[ch, pl.ds(rows[ch], c), :], sb[ch]) for ch in chs]
            qs = [_dot(qg_s[ch, pl.ds(rows[ch], c), :], sb[ch]) for ch in chs]
            vnb = [(u_s[ch, pl.ds(rows[ch], c), :] - ws[ch]).astype(BF16) for ch in chs]
            av = [_dot(a_s[ch, pl.ds(rows[ch], c), :], vnb[ch]) for ch in chs]
            kv = [_dot_tn(kd_s[ch, pl.ds(rows[ch], c), :], vnb[ch]) for ch in chs]
            for ch in chs:
                o_s[ch, pl.ds(rows[ch], c), :] = qs[ch] + av[ch]
                st_s[ch] = s[ch] * egl_s[ch, pl.ds(base // c + idx[ch], 1), :] + kv[ch]
            return carry
        lax.fori_loop(0, n // c, body, 0)

    scan(tc, 0)
    scan(tl, tc)

    nw = nw_ref[...]

    def finish(zg_ref, a_ref, n, base):
        def body(i, carry):
            r0 = pl.multiple_of(i * rt, rt)
            for r in range(2):
                y = o_s[r, pl.ds(base + r0, rt), :] + o_s[2 + r, pl.ds(base + r0, rt), :]
                yn = y * lax.rsqrt(jnp.mean(y * y, axis=-1, keepdims=True) + NORM_EPS) * nw
                g = zg_ref[pl.ds(r0, rt), r * hd:(r + 1) * hd].astype(F32)
                a_ref[pl.ds(r0, rt), r * hd:(r + 1) * hd] = (yn * _silu(g)).astype(BF16)
            return carry
        lax.fori_loop(0, n // rt, body, 0)

    finish(zgc_ref, ac_ref, tc, 0)
    finish(zgl_ref, al_ref, tl, tc)


def gdn_mixer(zc, zl, gtc, gtl, batch, conv_w, norm_w):
    tc = zc.shape[0] // batch
    tl = zl.shape[0] // batch
    hd = GDN_HEAD_DIM
    nqk = zc.shape[1] // hd // 6
    ttot = tc + tl
    kern = functools.partial(_gdn_kernel, tc=tc, tl=tl)

    def seg_specs(t):
        return [pl.BlockSpec((t, hd), lambda b, j: (b, j)),
                pl.BlockSpec((t, hd), lambda b, j: (b, nqk + j)),
                pl.BlockSpec((t, 2 * hd), lambda b, j: (b, nqk + j)),
                pl.BlockSpec((t, 2 * hd), lambda b, j: (b, 2 * nqk + j))]

    kw = conv_w.shape[0]
    ac, al = pl.pallas_call(
        kern,
        grid=(batch, nqk),
        in_specs=[pl.BlockSpec((1, 8, tc), lambda b, j: (b, j, 0)),
                  pl.BlockSpec((1, 8, tl), lambda b, j: (b, j, 0))]
                 + seg_specs(tc) + seg_specs(tl)
                 + [pl.BlockSpec((kw, hd), lambda b, j: (0, j)),
                    pl.BlockSpec((kw, hd), lambda b, j: (0, nqk + j)),
                    pl.BlockSpec((kw, 2 * hd), lambda b, j: (0, nqk + j)),
                    pl.BlockSpec((1, hd), lambda b, j: (0, 0))],
        out_specs=[pl.BlockSpec((tc, 2 * hd), lambda b, j: (b, j)),
                   pl.BlockSpec((tl, 2 * hd), lambda b, j: (b, j))],
        out_shape=[jax.ShapeDtypeStruct((batch * tc, 2 * nqk * hd), BF16),
                   jax.ShapeDtypeStruct((batch * tl, 2 * nqk * hd), BF16)],
        scratch_shapes=[pltpu.VMEM((max(tc, tl) + 16, 2 * hd), F32),
                        pltpu.VMEM((ttot, hd), F32), pltpu.VMEM((ttot, hd), F32),
                        pltpu.VMEM((ttot, 2 * hd), F32),
                        pltpu.VMEM((ttot, 128), F32),
                        pltpu.VMEM((4, ttot, hd), F32),
                        pltpu.VMEM((4, ttot, hd), BF16),
                        pltpu.VMEM((4, ttot, hd), BF16),
                        pltpu.VMEM((4, ttot, hd), BF16),
                        pltpu.VMEM((4, ttot, GDN_CHUNK), BF16),
                        pltpu.VMEM((4, ttot // GDN_CHUNK, hd), F32),
                        pltpu.VMEM((4, ttot, hd), F32),
                        pltpu.VMEM((4, hd, hd), F32)],
        compiler_params=_cparams(("parallel", "parallel")),
        name="gdn_mixer",
    )(gtc, gtl, zc, zc, zc, zc, zl, zl, zl, zl, conv_w, conv_w, conv_w, norm_w.reshape(1, hd))
    return al, ac


def gdn_gate_weights(w_ab, a_log, dt_bias):
    nv = a_log.shape[-1]
    nqk = nv // 2
    cols, alog, dtb = [], [], []
    for j in range(nqk):
        for kind in range(2):
            for d in range(2):
                for r in range(2):
                    hv = 2 * j + r
                    cols.append(w_ab[d][:, kind * nv + hv])
                    alog.append(a_log[d, hv] if kind == 0 else jnp.zeros((), F32))
                    dtb.append(dt_bias[d, hv] if kind == 0 else jnp.zeros((), F32))
    return jnp.stack(cols, axis=1), jnp.stack(alog)[:, None], jnp.stack(dtb)[:, None]


def _na_cases(rows):
    kh = min(WIN_H, rows)
    nbk = min(NA_KROWS, rows)
    cases, case_of_block, kstart = [], [], []
    for i in range(rows // NA_QROWS):
        r0 = i * NA_QROWS
        b0 = int(np.clip(r0 - kh // 2, 0, rows - nbk))
        sig = []
        for qr in range(NA_QROWS):
            q_row = r0 + qr
            row_start = int(np.clip(q_row - kh // 2, 0, rows - kh))
            for kr in range(nbk):
                k_row = b0 + kr
                ok = row_start <= k_row < row_start + kh
                dr = int(np.clip(k_row - q_row + WIN_H - 1, 0, 2 * WIN_H - 2))
                sig.append((ok, dr))
        sig = tuple(sig)
        if sig not in cases:
            cases.append(sig)
        case_of_block.append(cases.index(sig))
        kstart.append(b0 * GRID_W)
    return cases, case_of_block, kstart, nbk


def _na_bias_kernel(rpb_ref, o_ref, *, cases, nbk):
    h = pl.program_id(0)
    ndr, ndc = 2 * WIN_H - 1, 2 * WIN_W - 1
    kwid = min(WIN_W, GRID_W)
    qc = lax.broadcasted_iota(jnp.int32, (GRID_W, 128), 0)
    lane = lax.broadcasted_iota(jnp.int32, (GRID_W, 128), 1)
    kc = lane % GRID_W
    second = (lane // GRID_W).astype(F32)
    col_start = jnp.clip(qc - kwid // 2, 0, GRID_W - kwid)
    col_ok = (kc >= col_start) & (kc < col_start + kwid)
    dc = jnp.clip(kc - qc + WIN_W - 1, 0, ndc - 1)
    base = h * (ndr * ndc)
    cache = {}

    def tile_for(dr0, dr1):
        key = (dr0, dr1)
        if key not in cache:
            val = jnp.zeros((GRID_W, 128), F32)
            for n in range(ndc):
                s0 = rpb_ref[base + dr0 * ndc + n]
                s1 = rpb_ref[base + dr1 * ndc + n]
                val = jnp.where(dc == n, s0 + (s1 - s0) * second, val)
            cache[key] = val
        return cache[key]

    for ci, sig in enumerate(cases):
        for qr in range(NA_QROWS):
            for lt in range(nbk // 2):
                ok0, dr0 = sig[qr * nbk + 2 * lt]
                ok1, dr1 = sig[qr * nbk + 2 * lt + 1]
                if not (ok0 or ok1):
                    tile = jnp.full((GRID_W, 128), NEG_BIG, F32)
                else:
                    row_pen = (float(ok0) - 1.0) + (float(ok1) - float(ok0)) * second
                    tile = jnp.where(col_ok, tile_for(dr0, dr1), NEG_BIG) - row_pen * NEG_BIG
                o_ref[0, ci, qr * GRID_W:(qr + 1) * GRID_W, lt * 128:(lt + 1) * 128] = tile


def na_bias_tables(rpb, cases, nbk):
    nh = rpb.shape[0]
    ncase = len(cases)
    qblk = NA_QROWS * GRID_W
    return pl.pallas_call(
        functools.partial(_na_bias_kernel, cases=cases, nbk=nbk),
        grid=(nh,),
        in_specs=[pl.BlockSpec(memory_space=pltpu.SMEM)],
        out_specs=pl.BlockSpec((1, ncase, qblk, nbk * GRID_W), lambda h: (h, 0, 0, 0)),
        out_shape=jax.ShapeDtypeStruct((nh, ncase, qblk, nbk * GRID_W), F32),
        compiler_params=_cparams(("arbitrary",)),
        name="na_bias_tables",
    )(rpb.reshape(-1))


def _na_kernel(case_ref, kstart_ref, q_ref, k_ref, v_ref, qc_ref, kc_ref, vc_ref, bias_ref,
               ol_ref, oc_ref, *, nloc, scale, group):
    i = pl.program_id(2)
    kc = kc_ref[...]
    vc = vc_ref[...]

    @pl.when(i == 0)
    def _():
        s = _dot_nt(qc_ref[...], kc) * scale
        p = jnp.exp(s - jnp.max(s, axis=-1, keepdims=True))
        o = _dot(p.astype(BF16), vc) / jnp.sum(p, axis=-1, keepdims=True)
        oc_ref[...] = o.astype(BF16)

    gs = range(group)
    qblk = q_ref.shape[0] // group
    q = [q_ref[g * qblk:(g + 1) * qblk, :] for g in gs]
    ks = [pl.multiple_of(kstart_ref[i * group + g], GRID_W) for g in gs]
    s_loc = [_dot_nt(q[g], k_ref[pl.ds(ks[g], nloc), :]) * scale + bias_ref[0, case_ref[i * group + g]]
             for g in gs]
    s_ctx = [_dot_nt(q[g], kc) * scale for g in gs]
    m = [jnp.maximum(jnp.max(s_loc[g], axis=-1, keepdims=True), jnp.max(s_ctx[g], axis=-1, keepdims=True))
         for g in gs]
    p_loc = [jnp.exp(s_loc[g] - m[g]) for g in gs]
    p_ctx = [jnp.exp(s_ctx[g] - m[g]) for g in gs]
    denom = [jnp.sum(p_loc[g], axis=-1, keepdims=True) + jnp.sum(p_ctx[g], axis=-1, keepdims=True)
             for g in gs]
    pv = [_dot(p_loc[g].astype(BF16), v_ref[pl.ds(ks[g], nloc), :]) + _dot(p_ctx[g].astype(BF16), vc)
          for g in gs]
    for g in gs:
        ol_ref[g * qblk:(g + 1) * qblk, :] = (pv[g] / denom[g]).astype(BF16)


def na_mixer(zc, zl, batch, rpb):
    tc = zc.shape[0] // batch
    tl = zl.shape[0] // batch
    d = zc.shape[1] // 3
    nh = NA_HEADS
    hd = d // nh
    rows = tl // GRID_W
    cases, case_of_block, kstart, nbk = _na_cases(rows)
    bias = na_bias_tables(rpb, cases, nbk)
    qblk = NA_QROWS * GRID_W
    nblk = rows // NA_QROWS
    group = NA_GROUP if nblk % NA_GROUP == 0 else 1
    nstep = nblk // group
    nloc = nbk * GRID_W
    grid_spec = pltpu.PrefetchScalarGridSpec(
        num_scalar_prefetch=2,
        grid=(nh, batch, nstep),
        in_specs=[pl.BlockSpec((group * qblk, hd), lambda h, b, i, *_: (b * nstep + i, h)),
                  pl.BlockSpec((tl, hd), lambda h, b, i, *_: (b, nh + h)),
                  pl.BlockSpec((tl, hd), lambda h, b, i, *_: (b, 2 * nh + h)),
                  pl.BlockSpec((tc, hd), lambda h, b, i, *_: (b, h)),
                  pl.BlockSpec((tc, hd), lambda h, b, i, *_: (b, nh + h)),
                  pl.BlockSpec((tc, hd), lambda h, b, i, *_: (b, 2 * nh + h)),
                  pl.BlockSpec((1, len(cases), qblk, nloc), lambda h, b, i, *_: (h, 0, 0, 0))],
        out_specs=[pl.BlockSpec((group * qblk, hd), lambda h, b, i, *_: (b * nstep + i, h)),
                   pl.BlockSpec((tc, hd), lambda h, b, i, *_: (b, h))])
    ol, oc = pl.pallas_call(
        functools.partial(_na_kernel, nloc=nloc, scale=hd ** -0.5, group=group),
        grid_spec=grid_spec,
        out_shape=[jax.ShapeDtypeStruct((batch * tl, d), BF16),
                   jax.ShapeDtypeStruct((batch * tc, d), BF16)],
        compiler_params=_cparams(("parallel", "parallel", "arbitrary")),
        name="na_mixer",
    )(jnp.asarray(case_of_block, jnp.int32), jnp.asarray(kstart, jnp.int32),
      zl, zl, zl, zc, zc, zc, bias)
    return ol, oc


TM = 512
TN_PROJ = 1024
TN_OUT = 512
TF_MLP = 1024
TN_MLP = 256


def kernel(x, c, ctx, c_ctx, ada_w, ada_b, norm_g, mlp_w1, mlp_w2, ret_w_in, ret_w_out, gdn_w_in, gdn_conv_w, gdn_w_ab, gdn_a_log, gdn_dt_bias, gdn_norm_w, gdn_w_out, na_w_in, na_w_out, na_rpb):
    batch, t, d = x.shape
    tctx = ctx.shape[1]
    depth = ada_w.shape[0]

    nrow = -(-(batch + 1) // 16) * 16
    cvec = jnp.zeros((nrow, d), F32).at[0].set(c_ctx).at[1:batch + 1].set(c)
    mods = ada_modulation(cvec, ada_w, ada_b).reshape(depth, nrow, N_MOD, d)

    xl = x.reshape(batch * t, d)
    xc = ctx.reshape(batch * tctx, d)
    ret_lg, ret_cos, ret_sin = retention_tables(t, ret_w_in.shape[2] // 6 // RET_HEADS)

    i_ret = i_gdn = i_na = 0
    for i in range(depth):
        last = i == depth - 1
        m_ctx = mods[i, 0:1]
        m_lat = mods[i, 1:batch + 1]
        g = norm_g[i]
        kind = i % 3

        def project(w, out_dtype=BF16, tn=TN_PROJ):
            zc = norm_mod_matmul(xc, m_ctx, batch * tctx, g[0], w, 0, 1, out_dtype, TM, tn)
            zl = norm_mod_matmul(xl, m_lat, t, g[0], w, 0, 1, out_dtype, TM, tn)
            return zc, zl

        if kind == 0:
            zc, zl = project(ret_w_in[i_ret].astype(BF16))
            al, ac = retention_mixer(zc, zl, batch, ret_lg, ret_cos, ret_sin, not last)
            w_out = ret_w_out[i_ret].astype(BF16)
            i_ret += 1
        elif kind == 1:
            zc, zl = project(gdn_w_in[i_gdn].astype(BF16))
            w_gate, alog_col, dtb_col = gdn_gate_weights(gdn_w_ab[i_gdn], gdn_a_log[i_gdn],
                                                        gdn_dt_bias[i_gdn])
            abc, abl = project(w_gate.astype(BF16), out_dtype=F32)
            gtc = gdn_gates(abc.reshape(batch, tctx, -1).transpose(0, 2, 1), alog_col, dtb_col)
            gtl = gdn_gates(abl.reshape(batch, t, -1).transpose(0, 2, 1), alog_col, dtb_col)
            al, ac = gdn_mixer(zc, zl, gtc, gtl, batch, gdn_conv_w[i_gdn], gdn_norm_w[i_gdn])
            w_out = gdn_w_out[i_gdn].astype(BF16)
            i_gdn += 1
        else:
            zc, zl = project(na_w_in[i_na].astype(BF16))
            al, ac = na_mixer(zc, zl, batch, na_rpb[i_na])
            w_out = na_w_out[i_na].astype(BF16)
            i_na += 1

        w1 = mlp_w1[i].astype(BF16)
        w2 = mlp_w2[i].astype(BF16)
        xl = out_proj_residual(al, w_out, xl, m_lat, t, g[1], 2, TM, TN_OUT)
        xl = mlp_residual(xl, m_lat, t, g[2], g[3], w1, w2, TM, TF_MLP, TN_MLP)
        if not last:
            xc = out_proj_residual(ac, w_out, xc, m_ctx, batch * tctx, g[1], 2, TM, TN_OUT)
            xc = mlp_residual(xc, m_ctx, batch * tctx, g[2], g[3], w1, w2, TM, TF_MLP, TN_MLP)
    return xl.reshape(batch, t, d)
```

```python
import functools
import math

import numpy as np
import jax
import jax.numpy as jnp
from jax import lax
from jax.experimental import pallas as pl
from jax.experimental.pallas import tpu as pltpu

F32 = jnp.float32
BF16 = jnp.bfloat16

NORM_EPS = 1e-6
L2_EPS = 1e-6
ROPE_BASE = 10000.0
N_MOD = 6
GRID_W = 64

RET_HEADS = 8
RET_CHUNK = 256
GDN_HEAD_DIM = 128
GDN_CHUNK = 128
GDN_CONV = 5
NA_HEADS = 16
WIN_H = 8
WIN_W = 16
NA_QROWS = 2
NA_KROWS = 10
NEG_BIG = -1e30
NA_GROUP = 4

V7X_VMEM_LIMIT_BYTES = 56 * 1024 * 1024


def _cparams(sem):
    return pltpu.CompilerParams(dimension_semantics=sem, vmem_limit_bytes=V7X_VMEM_LIMIT_BYTES)


def _sigmoid(x):
    return 1.0 / (1.0 + jnp.exp(-x))


def _silu(x):
    return x * _sigmoid(x)


def _dot(a, b):
    return jnp.dot(a, b, preferred_element_type=F32)


def _dot_nt(a, b):
    return lax.dot_general(a, b, (((1,), (1,)), ((), ())), preferred_element_type=F32)


def _dot_tn(a, b):
    return lax.dot_general(a, b, (((0,), (0,)), ((), ())), preferred_element_type=F32)


def _rms(y, g):
    return y * lax.rsqrt(jnp.mean(y * y, axis=-1, keepdims=True) + NORM_EPS) * g


def _ada_kernel(c_ref, w_ref, b_ref, o_ref):
    s = _silu(c_ref[...]).astype(BF16)
    o_ref[0] = _dot(s, w_ref[0].astype(BF16)) + b_ref[0]


def ada_modulation(cvec, ada_w, ada_b, tn=1024):
    depth, d, n = ada_w.shape
    r = cvec.shape[0]
    return pl.pallas_call(
        _ada_kernel,
        grid=(depth, n // tn),
        in_specs=[pl.BlockSpec((r, d), lambda l, j: (0, 0)),
                  pl.BlockSpec((1, d, tn), lambda l, j: (l, 0, j)),
                  pl.BlockSpec((1, 1, tn), lambda l, j: (l, 0, j))],
        out_specs=pl.BlockSpec((1, r, tn), lambda l, j: (l, 0, j)),
        out_shape=jax.ShapeDtypeStruct((depth, r, n), F32),
        compiler_params=_cparams(("parallel", "parallel")),
        name="ada_modulation",
    )(cvec, ada_w, ada_b.reshape(depth, 1, n))


def _modulated_norm(x, g, m_ref, shift_idx, scale_idx):
    y = _rms(x, g)
    return y * (1.0 + m_ref[0, scale_idx:scale_idx + 1, :]) + m_ref[0, shift_idx:shift_idx + 1, :]


def _nmm_kernel(x_ref, m_ref, g_ref, w_ref, o_ref, h_ref, *, shift_idx, scale_idx):
    @pl.when(pl.program_id(1) == 0)
    def _():
        h_ref[...] = _modulated_norm(x_ref[...], g_ref[...], m_ref, shift_idx, scale_idx).astype(BF16)

    o_ref[...] = _dot(h_ref[...], w_ref[0]).astype(o_ref.dtype)


def norm_mod_matmul(x, mods, tokens_per_mod, g, w_slabs, shift_idx, scale_idx, out_dtype, tm):
    n, d = x.shape
    nj, _, tn = w_slabs.shape
    nout = nj * tn
    w = w_slabs
    tm = min(tm, tokens_per_mod)
    bpm = tokens_per_mod // tm
    return pl.pallas_call(
        functools.partial(_nmm_kernel, shift_idx=shift_idx, scale_idx=scale_idx),
        grid=(n // tm, nout // tn),
        in_specs=[pl.BlockSpec((tm, d), lambda i, j: (i, 0)),
                  pl.BlockSpec((1, N_MOD, d), lambda i, j: (i // bpm, 0, 0)),
                  pl.BlockSpec((1, d), lambda i, j: (0, 0)),
                  pl.BlockSpec((1, d, tn), lambda i, j: (j, 0, 0))],
        out_specs=pl.BlockSpec((tm, tn), lambda i, j: (i, j)),
        out_shape=jax.ShapeDtypeStruct((n, nout), out_dtype),
        scratch_shapes=[pltpu.VMEM((tm, d), BF16)],
        compiler_params=_cparams(("parallel", "arbitrary")),
        name="norm_mod_matmul",
    )(x, mods, g.reshape(1, d), w)


def _rms_residual_slabs(y_ref, x_ref, gate, g_ref, o_ref):
    ns, _, tn = y_ref.shape
    ss = jnp.sum(y_ref[0] * y_ref[0], axis=-1, keepdims=True)
    for j in range(1, ns):
        ss = ss + jnp.sum(y_ref[j] * y_ref[j], axis=-1, keepdims=True)
    inv = lax.rsqrt(ss / (ns * tn) + NORM_EPS)
    for j in range(ns):
        sl = slice(j * tn, (j + 1) * tn)
        o_ref[:, sl] = x_ref[:, sl] + gate[:, sl] * (y_ref[j] * inv * g_ref[:, sl])


def _out_res_kernel(a_ref, w_ref, x_ref, m_ref, g_ref, o_ref, y_ref, *, gate_idx, nj):
    for j in range(nj):
        y_ref[j] = _dot(a_ref[...], w_ref[j])
    _rms_residual_slabs(y_ref, x_ref, m_ref[0, gate_idx:gate_idx + 1, :], g_ref, o_ref)


def column_slabs(w, tn):
    k, n = w.shape
    return w.reshape(k, n // tn, tn).transpose(1, 0, 2)


def out_proj_residual(a, w_slabs, x, mods, tokens_per_mod, g, gate_idx, tm):
    n, kdim = a.shape
    nj, _, tn = w_slabs.shape
    d = nj * tn
    tm = min(tm, tokens_per_mod)
    bpm = tokens_per_mod // tm
    w = w_slabs
    return pl.pallas_call(
        functools.partial(_out_res_kernel, gate_idx=gate_idx, nj=nj),
        grid=(n // tm,),
        in_specs=[pl.BlockSpec((tm, kdim), lambda i: (i, 0)),
                  pl.BlockSpec((nj, kdim, tn), lambda i: (0, 0, 0), pipeline_mode=pl.Buffered(1)),
                  pl.BlockSpec((tm, d), lambda i: (i, 0)),
                  pl.BlockSpec((1, N_MOD, d), lambda i: (i // bpm, 0, 0)),
                  pl.BlockSpec((1, d), lambda i: (0, 0))],
        out_specs=pl.BlockSpec((tm, d), lambda i: (i, 0)),
        out_shape=jax.ShapeDtypeStruct((n, d), F32),
        scratch_shapes=[pltpu.VMEM((nj, tm, tn), F32)],
        compiler_params=_cparams(("parallel",)),
        name="out_proj_residual",
    )(a, w, x, mods, g.reshape(1, d))


def _mlp_kernel(x_ref, m_ref, g2_ref, g3_ref, w1_ref, w2_ref, o_ref, h_ref, a_ref, y_ref, *, n1, n2, tf):
    k = pl.program_id(1)

    @pl.when(k == 0)
    def _():
        h_ref[...] = _modulated_norm(x_ref[...], g2_ref[...], m_ref, 3, 4).astype(BF16)

    @pl.when(k < n1)
    def _():
        a = jnp.maximum(_dot(h_ref[...], w1_ref[0]), 0.0)
        a_ref[jnp.minimum(k, n1 - 1)] = (a * a).astype(BF16)

    @pl.when(k >= n1)
    def _():
        acc = _dot(a_ref[0], w2_ref[0, 0:tf, :])
        for s in range(1, n1):
            acc = acc + _dot(a_ref[s], w2_ref[0, s * tf:(s + 1) * tf, :])
        y_ref[jnp.maximum(k - n1, 0)] = acc

    @pl.when(k == n1 + n2 - 1)
    def _():
        _rms_residual_slabs(y_ref, x_ref, m_ref[0, 5:6, :], g3_ref, o_ref)


def mlp_residual(x, mods, tokens_per_mod, g2, g3, w1_slabs, w2_slabs, tm):
    n, d = x.shape
    n1, _, tf = w1_slabs.shape
    n2, dff, tn = w2_slabs.shape
    w1, w2 = w1_slabs, w2_slabs
    tm = min(tm, tokens_per_mod)
    bpm = tokens_per_mod // tm
    return pl.pallas_call(
        functools.partial(_mlp_kernel, n1=n1, n2=n2, tf=tf),
        grid=(n // tm, n1 + n2),
        in_specs=[pl.BlockSpec((tm, d), lambda i, k: (i, 0)),
                  pl.BlockSpec((1, N_MOD, d), lambda i, k: (i // bpm, 0, 0)),
                  pl.BlockSpec((1, d), lambda i, k: (0, 0)),
                  pl.BlockSpec((1, d), lambda i, k: (0, 0)),
                  pl.BlockSpec((1, d, tf), lambda i, k: (jnp.minimum(k, n1 - 1), 0, 0)),
                  pl.BlockSpec((1, dff, tn), lambda i, k: (jnp.maximum(k - n1, 0), 0, 0))],
        out_specs=pl.BlockSpec((tm, d), lambda i, k: (i, 0)),
        out_shape=jax.ShapeDtypeStruct((n, d), F32),
        scratch_shapes=[pltpu.VMEM((tm, d), BF16), pltpu.VMEM((n1, tm, tf), BF16),
                        pltpu.VMEM((n2, tm, tn), F32)],
        compiler_params=_cparams(("parallel", "arbitrary")),
        name="mlp_residual",
    )(x, mods, g2.reshape(1, d), g3.reshape(1, d), w1, w2)


def _ret_kernel(lg_ref, cos_ref, sin_ref, qc_ref, kc_ref, vc_ref, gc_ref, ql_ref, kl_ref, vl_ref, gl_ref,
                *rest, tc, tl, dk, dv, with_ctx_out):
    if with_ctx_out:
        ac_ref, al_ref, qs_ref, ks_ref, of_ref, ob_ref, rf_ref, rb_ref = rest
    else:
        al_ref, qs_ref, ks_ref, of_ref, ob_ref, rf_ref, rb_ref = rest
        ac_ref = None
    c = RET_CHUNK
    half = dk // 2
    k_scale = dk ** -0.5

    qs_ref[0:tc, :] = qc_ref[...]
    ks_ref[0:tc, :] = (kc_ref[...].astype(F32) * k_scale).astype(BF16)

    rt = 256

    def rope_body(i, carry):
        r0 = pl.multiple_of(i * rt, rt)
        cs = cos_ref[pl.ds(r0, rt), :]
        sn = sin_ref[pl.ds(r0, rt), :]
        for src, dst, scale in ((ql_ref, qs_ref, 1.0), (kl_ref, ks_ref, k_scale)):
            t = src[pl.ds(r0, rt), :].astype(F32) * scale
            t1, t2 = t[:, :half], t[:, half:]
            dst[pl.ds(tc + r0, rt), :] = jnp.concatenate(
                [t1 * cs - t2 * sn, t1 * sn + t2 * cs], axis=-1).astype(BF16)
        return carry

    lax.fori_loop(0, tl // rt, rope_body, 0)

    lgf = lg_ref[0, 0:1, 0:1]
    lgb = lg_ref[0, 1:2, 0:1]
    ci = lax.broadcasted_iota(jnp.int32, (c, c), 0)
    mi = lax.broadcasted_iota(jnp.int32, (c, c), 1)
    dist = (ci - mi).astype(F32)
    intra_f = jnp.where(dist >= 0, jnp.exp(lgf * jnp.maximum(dist, 0.0)), 0.0)
    intra_b = jnp.where(dist <= 0, jnp.exp(lgb * jnp.maximum(-dist, 0.0)), 0.0)
    pos = lax.broadcasted_iota(jnp.int32, (c, 1), 0).astype(F32)
    xi_f = jnp.exp(lgf * (pos + 1.0))
    zeta_f = jnp.exp(lgf * (c - 1.0 - pos))
    xi_b = jnp.exp(lgb * (c - pos))
    zeta_b = jnp.exp(lgb * pos)
    cd_f = jnp.exp(lgf * float(c))
    cd_b = jnp.exp(lgb * float(c))

    rf_ref[...] = jnp.zeros_like(rf_ref)
    rb_ref[...] = jnp.zeros_like(rb_ref)
    dirs = ((rf_ref, of_ref, intra_f, xi_f, zeta_f, cd_f), (rb_ref, ob_ref, intra_b, xi_b, zeta_b, cd_b))

    def make_body(v_ref, base, n):
        def body(i, carry):
            loc = (pl.multiple_of(i * c, c), pl.multiple_of((n - 1 - i) * c, c))
            qt = [qs_ref[pl.ds(base + l, c), :] for l in loc]
            kt = [ks_ref[pl.ds(base + l, c), :] for l in loc]
            vt = [v_ref[pl.ds(l, c), :] for l in loc]
            r = [dr[0][...] for dr in dirs]
            s = [_dot_nt(q, k) for q, k in zip(qt, kt)]
            qr = [_dot(q, x.astype(BF16)) for q, x in zip(qt, r)]
            kv = [_dot_tn(k, (v.astype(F32) * dr[4]).astype(BF16)) for k, v, dr in zip(kt, vt, dirs)]
            sv = [_dot((x * dr[2]).astype(BF16), v) for x, v, dr in zip(s, vt, dirs)]
            for d, dr in enumerate(dirs):
                dr[1][pl.ds(base + loc[d], c), :] = sv[d] + qr[d] * dr[3]
                dr[0][...] = r[d] * dr[5] + kv[d]
            return carry
        return body

    lax.fori_loop(0, tc // c, make_body(vc_ref, 0, tc // c), 0)
    lax.fori_loop(0, tl // c, make_body(vl_ref, tc, tl // c), 0)

    def finish(g_ref, a_ref, base, n):
        def body(i, carry):
            r0 = pl.multiple_of(i * rt, rt)
            y = of_ref[pl.ds(base + r0, rt), :] + ob_ref[pl.ds(base + r0, rt), :]
            yn = y * lax.rsqrt(jnp.mean(y * y, axis=-1, keepdims=True) + NORM_EPS)
            g = g_ref[pl.ds(r0, rt), :].astype(F32)
            a_ref[pl.ds(r0, rt), :] = (_silu(g) * yn).astype(BF16)
            return carry
        lax.fori_loop(0, n // rt, body, 0)

    if with_ctx_out:
        finish(gc_ref, ac_ref, 0, tc)
    finish(gl_ref, al_ref, tc, tl)


def retention_mixer(zc, zl, batch, lg, cos, sin, with_ctx_out):
    tc = zc.shape[0] // batch
    tl = zl.shape[0] // batch
    width = zc.shape[1]
    dk = width // 6 // RET_HEADS
    dv = 2 * dk
    h = RET_HEADS
    kern = functools.partial(_ret_kernel, tc=tc, tl=tl, dk=dk, dv=dv, with_ctx_out=with_ctx_out)

    def seg_specs(t):
        return [pl.BlockSpec((t, dk), lambda b, j: (b, j)),
                pl.BlockSpec((t, dk), lambda b, j: (b, h + j)),
                pl.BlockSpec((t, dv), lambda b, j: (b, h + j)),
                pl.BlockSpec((t, dv), lambda b, j: (b, 2 * h + j))]

    out_shape = [jax.ShapeDtypeStruct((batch * tl, h * dv), BF16)]
    out_specs = [pl.BlockSpec((tl, dv), lambda b, j: (b, j))]
    if with_ctx_out:
        out_shape = [jax.ShapeDtypeStruct((batch * tc, h * dv), BF16)] + out_shape
        out_specs = [pl.BlockSpec((tc, dv), lambda b, j: (b, j))] + out_specs
    outs = pl.pallas_call(
        kern,
        grid=(batch, h),
        in_specs=[pl.BlockSpec((1, 8, 128), lambda b, j: (j, 0, 0)),
                  pl.BlockSpec((tl, dk // 2), lambda b, j: (0, 0)),
                  pl.BlockSpec((tl, dk // 2), lambda b, j: (0, 0))] + seg_specs(tc) + seg_specs(tl),
        out_specs=out_specs,
        out_shape=out_shape,
        scratch_shapes=[pltpu.VMEM((tc + tl, dk), BF16), pltpu.VMEM((tc + tl, dk), BF16),
                        pltpu.VMEM((tc + tl, dv), F32), pltpu.VMEM((tc + tl, dv), F32),
                        pltpu.VMEM((dk, dv), F32), pltpu.VMEM((dk, dv), F32)],
        compiler_params=_cparams(("parallel", "parallel")),
        name="retention_mixer",
    )(lg, cos, sin, zc, zc, zc, zc, zl, zl, zl, zl)
    if with_ctx_out:
        return outs[1], outs[0]
    return outs[0], None


def retention_tables(t, dk):
    fwd = jnp.log(1.0 - 2.0 ** (-5.0 - jnp.arange(RET_HEADS, dtype=F32)))
    lg = jnp.zeros((RET_HEADS, 8, 128), F32)
    lg = lg.at[:, 0, :].set(fwd[:, None]).at[:, 1, :].set(fwd[::-1][:, None])
    tt = jnp.arange(t)
    row = (tt // GRID_W).astype(F32)
    col = (tt % GRID_W).astype(F32)
    n_pairs = dk // 2
    inv = ROPE_BASE ** (-jnp.arange(0, n_pairs, 2, dtype=F32) / n_pairs)
    ang = jnp.concatenate([row[:, None] * inv, col[:, None] * inv], axis=-1)
    return lg, jnp.cos(ang), jnp.sin(ang)


def _split3(x):
    hi = x.astype(BF16)
    r1 = x - hi.astype(F32)
    mid = r1.astype(BF16)
    lo = (r1 - mid.astype(F32)).astype(BF16)
    return hi, mid, lo


def _gdn_gates_kernel(ab_ref, alog_ref, dtb_ref, o_ref, *, t):
    c = GDN_CHUNK
    nrow = ab_ref.shape[1]
    rowid = lax.broadcasted_iota(jnp.int32, (nrow, 1), 0) % 8
    mi = lax.broadcasted_iota(jnp.int32, (c, c), 0)
    ci = lax.broadcasted_iota(jnp.int32, (c, c), 1)
    pre = (mi <= ci).astype(BF16)
    suf = (mi >= ci).astype(BF16)
    a_neg = -jnp.exp(alog_ref[...])
    dtb = dtb_ref[...]
    for w in range(t // c):
        x = ab_ref[0, :, w * c:(w + 1) * c]
        z = x + dtb
        g = a_neg * (jnp.maximum(z, 0.0) + jnp.log1p(jnp.exp(-jnp.abs(z))))
        parts = _split3(g)
        cp = sum(_dot(p, pre) for p in parts)
        cs = sum(_dot(p, suf) for p in parts)
        o_ref[0, :, w * c:(w + 1) * c] = jnp.where(rowid < 2, cp, jnp.where(rowid < 4, cs, _sigmoid(x)))


def gdn_gates(ab_t, alog_col, dtb_col):
    b, nrow, t = ab_t.shape
    return pl.pallas_call(
        functools.partial(_gdn_gates_kernel, t=t),
        grid=(b,),
        in_specs=[pl.BlockSpec((1, nrow, t), lambda i: (i, 0, 0)),
                  pl.BlockSpec((nrow, 1), lambda i: (0, 0)),
                  pl.BlockSpec((nrow, 1), lambda i: (0, 0))],
        out_specs=pl.BlockSpec((1, nrow, t), lambda i: (i, 0, 0)),
        out_shape=jax.ShapeDtypeStruct((b, nrow, t), F32),
        compiler_params=_cparams(("parallel",)),
        name="gdn_gates",
    )(ab_t, alog_col, dtb_col)


def _dot_each(a_list, b_list):
    return [_dot(a.astype(BF16), b.astype(BF16)) for a, b in zip(a_list, b_list)]


GDN_INV_BASE = 16
GDN_PREP_CHUNKS = 2


def _unit_triangular_inverse_minus_eye(ms, ri, li):
    n = ms[0].shape[0]
    s = GDN_INV_BASE
    inblk = (ri // s) == (li // s)
    ps = [jnp.where(inblk, -m, 0.0) for m in ms]
    qs = ps
    ps = _dot_each(ps, ps)
    level = 2
    while 2 * level < s:
        both = _dot_each([jnp.concatenate([q, p], axis=0) for q, p in zip(qs, ps)], ps)
        qs = [q + p + b[:n] for q, p, b in zip(qs, ps, both)]
        ps = [b[n:] for b in both]
        level *= 2
    qp = _dot_each(qs, ps)
    eye = jnp.where(ri == li, 1.0, 0.0)
    xs = [eye + q + p + t for q, p, t in zip(qs, ps, qp)]
    while s < n:
        off = ((ri // (2 * s)) == (li // (2 * s))) & ((ri // s) != (li // s))
        cxs = _dot_each([jnp.where(off, m, 0.0) for m in ms], xs)
        xcx = _dot_each(xs, cxs)
        xs = [x - t for x, t in zip(xs, xcx)]
        s *= 2
    return [x - eye for x in xs]


def _gdn_kernel(gtc_ref, gtl_ref, zqc_ref, zkc_ref, zvc_ref, zgc_ref, zql_ref, zkl_ref, zvl_ref, zgl_ref,
                cwq_ref, cwk_ref, cwv_ref, nw_ref, ac_ref, al_ref,
                pad_ref, q_s, k_s, v_s, gcol_s, u_s, w_s, qg_s, kd_s, a_s, egl_s, o_s, st_s, *, tc, tl):
    c = GDN_CHUNK
    hd = GDN_HEAD_DIM
    ttot = tc + tl
    rt = 256
    halo = 8
    kw = GDN_CONV
    lead = halo - (kw - 1) // 2

    def conv_seg(z_ref, cw_ref, n, width, dst_ref, base, l2_scale):
        zero = jnp.zeros((halo, width), F32)
        pad_ref[0:halo, 0:width] = zero
        pad_ref[halo + n:2 * halo + n, 0:width] = zero

        def fill(i, carry):
            r0 = pl.multiple_of(i * rt, rt)
            pad_ref[pl.ds(halo + r0, rt), 0:width] = z_ref[pl.ds(r0, rt), :].astype(F32)
            return carry
        lax.fori_loop(0, n // rt, fill, 0)
        cw = cw_ref[...]

        def body(i, carry):
            r0 = pl.multiple_of(i * rt, rt)
            xx = pad_ref[pl.ds(r0, rt + 2 * halo), 0:width]
            acc = xx[lead:lead + rt, :] * cw[0:1, :]
            for k in range(1, kw):
                acc = acc + xx[lead + k:lead + k + rt, :] * cw[k:k + 1, :]
            y = _silu(acc)
            if l2_scale is not None:
                for j in range(width // hd):
                    yj = y[:, j * hd:(j + 1) * hd]
                    yj = yj * lax.rsqrt(jnp.sum(yj * yj, axis=-1, keepdims=True) + L2_EPS) * l2_scale
                    dst_ref[pl.ds(base + r0, rt), j * hd:(j + 1) * hd] = yj
            else:
                dst_ref[pl.ds(base + r0, rt), :] = y
            return carry
        lax.fori_loop(0, n // rt, body, 0)

    for z_ref, n, base in ((zqc_ref, tc, 0), (zql_ref, tl, tc)):
        conv_seg(z_ref, cwq_ref, n, hd, q_s, base, hd ** -0.5)
    for z_ref, n, base in ((zkc_ref, tc, 0), (zkl_ref, tl, tc)):
        conv_seg(z_ref, cwk_ref, n, hd, k_s, base, 1.0)
    for z_ref, n, base in ((zvc_ref, tc, 0), (zvl_ref, tl, tc)):
        conv_seg(z_ref, cwv_ref, n, 2 * hd, v_s, base, None)

    def gate_cols(gt_ref, n, base):
        def body(i, carry):
            r0 = pl.multiple_of(i * c, c)
            tile = gt_ref[0, :, pl.ds(r0, c)]
            gcol_s[pl.ds(base + r0, c), :] = jnp.concatenate([tile] * (c // 8), axis=0).T
            return carry
        lax.fori_loop(0, n // c, body, 0)

    gate_cols(gtc_ref, tc, 0)
    gate_cols(gtl_ref, tl, tc)

    ri = lax.broadcasted_iota(jnp.int32, (c, c), 0)
    li = lax.broadcasted_iota(jnp.int32, (c, c), 1)

    def prep(gt_ref, n, base, cpi):
        def body(i, carry):
            rows, chans, mmats, rhss = [], [], [], []
            for cc in range(cpi):
                r0 = pl.multiple_of((i * cpi + cc) * c, c)
                row = base + r0
                g8 = gt_ref[0, :, pl.ds(r0, c)]
                gc8 = gcol_s[pl.ds(row, c), :]
                kb = k_s[pl.ds(row, c), :]
                qb = q_s[pl.ds(row, c), :]
                kbb = kb.astype(BF16)
                kk = _dot_nt(kbb, kbb)
                qk = _dot_nt(qb.astype(BF16), kbb)
                for d in range(2):
                    incl = (ri >= li) if d == 0 else (ri <= li)
                    strict = (ri > li) if d == 0 else (ri < li)
                    for r in range(2):
                        ch = 2 * d + r
                        g_row = g8[ch:ch + 1, :]
                        g_b = jnp.broadcast_to(gc8[:, ch:ch + 1], (c, hd))
                        beta_b = jnp.broadcast_to(gc8[:, 4 + ch:5 + ch], (c, hd))
                        eg_b = jnp.exp(g_b)
                        decay = jnp.exp(jnp.where(incl, g_b - g_row, -jnp.inf))
                        mmats.append(jnp.where(strict, kk * decay, 0.0) * beta_b)
                        rows.append(row)
                        chans.append(ch)
                        qg_s[ch, pl.ds(row, c), :] = (qb * eg_b).astype(BF16)
                        a_s[ch, pl.ds(row, c), :] = (qk * decay).astype(BF16)
                        g_last = g_row[:, c - 1:c] if d == 0 else g_row[:, 0:1]
                        kd_s[ch, pl.ds(row, c), :] = (kb * jnp.exp(g_last - g_b)).astype(BF16)
                        egl_s[ch, pl.ds(base // c + i * cpi + cc, 1), :] = jnp.broadcast_to(
                            jnp.exp(g_last), (1, hd))
                        vb = v_s[pl.ds(row, c), r * hd:(r + 1) * hd] * beta_b
                        rhss.append(jnp.concatenate([vb, kb * (beta_b * eg_b)], axis=1))
            tm1s = _unit_triangular_inverse_minus_eye(mmats, ri, li)
            for row, ch, rhs, corr in zip(rows, chans, rhss, _dot_each(tm1s, rhss)):
                sol = rhs + corr
                u_s[ch, pl.ds(row, c), :] = sol[:, :hd]
                w_s[ch, pl.ds(row, c), :] = sol[:, hd:].astype(BF16)
            return carry
        lax.fori_loop(0, n // (c * cpi), body, 0)

    prep(gtc_ref, tc, 0, GDN_PREP_CHUNKS)
    prep(gtl_ref, tl, tc, GDN_PREP_CHUNKS)

    st_s[...] = jnp.zeros_like(st_s)

    def scan(n, base):
        def body(i, carry):
            chs = range(4)
            idx = [i if ch < 2 else n // c - 1 - i for ch in chs]
            rows = [pl.multiple_of(base + idx[ch] * c, c) for ch in chs]
            s = [st_s[ch] for ch in chs]
            sb = [x.astype(BF16) for x in s]
            ws = [_dot(w_s[ch, pl.ds(rows[ch], c), :], sb[ch]) for ch in chs]
            qs = [_dot(qg_s[ch, pl.ds(rows[ch], c), :], sb[ch]) for ch in chs]
            vnb = [(u_s[ch, pl.ds(rows[ch], c), :] - ws[ch]).astype(BF16) for ch in chs]
            av = [_dot(a_s[ch, pl.ds(rows[ch], c), :], vnb[ch]) for ch in chs]
            kv = [_dot_tn(kd_s[ch, pl.ds(rows[ch], c), :], vnb[ch]) for ch in chs]
            for ch in chs:
                o_s[ch, pl.ds(rows[ch], c), :] = qs[ch] + av[ch]
                st_s[ch] = s[ch] * egl_s[ch, pl.ds(base // c + idx[ch], 1), :] + kv[ch]
            return carry
        lax.fori_loop(0, n // c, body, 0)

    scan(tc, 0)
    scan(tl, tc)

    nw = nw_ref[...]

    def finish(zg_ref, a_ref, n, base):
        def body(i, carry):
            r0 = pl.multiple_of(i * rt, rt)
            for r in range(2):
                y = o_s[r, pl.ds(base + r0, rt), :] + o_s[2 + r, pl.ds(base + r0, rt), :]
                yn = y * lax.rsqrt(jnp.mean(y * y, axis=-1, keepdims=True) + NORM_EPS) * nw
                g = zg_ref[pl.ds(r0, rt), r * hd:(r + 1) * hd].astype(F32)
                a_ref[pl.ds(r0, rt), r * hd:(r + 1) * hd] = (yn * _silu(g)).astype(BF16)
            return carry
        lax.fori_loop(0, n // rt, body, 0)

    finish(zgc_ref, ac_ref, tc, 0)
    finish(zgl_ref, al_ref, tl, tc)


def gdn_mixer(zc, zl, gtc, gtl, batch, conv_w, norm_w):
    tc = zc.shape[0] // batch
    tl = zl.shape[0] // batch
    hd = GDN_HEAD_DIM
    nqk = zc.shape[1] // hd // 6
    ttot = tc + tl
    kern = functools.partial(_gdn_kernel, tc=tc, tl=tl)

    def seg_specs(t):
        return [pl.BlockSpec((t, hd), lambda b, j: (b, j)),
                pl.BlockSpec((t, hd), lambda b, j: (b, nqk + j)),
                pl.BlockSpec((t, 2 * hd), lambda b, j: (b, nqk + j)),
                pl.BlockSpec((t, 2 * hd), lambda b, j: (b, 2 * nqk + j))]

    kw = conv_w.shape[0]
    ac, al = pl.pallas_call(
        kern,
        grid=(batch, nqk),
        in_specs=[pl.BlockSpec((1, 8, tc), lambda b, j: (b, j, 0)),
                  pl.BlockSpec((1, 8, tl), lambda b, j: (b, j, 0))]
                 + seg_specs(tc) + seg_specs(tl)
                 + [pl.BlockSpec((kw, hd), lambda b, j: (0, j)),
                    pl.BlockSpec((kw, hd), lambda b, j: (0, nqk + j)),
                    pl.BlockSpec((kw, 2 * hd), lambda b, j: (0, nqk + j)),
                    pl.BlockSpec((1, hd), lambda b, j: (0, 0))],
        out_specs=[pl.BlockSpec((tc, 2 * hd), lambda b, j: (b, j)),
                   pl.BlockSpec((tl, 2 * hd), lambda b, j: (b, j))],
        out_shape=[jax.ShapeDtypeStruct((batch * tc, 2 * nqk * hd), BF16),
                   jax.ShapeDtypeStruct((batch * tl, 2 * nqk * hd), BF16)],
        scratch_shapes=[pltpu.VMEM((max(tc, tl) + 16, 2 * hd), F32),
                        pltpu.VMEM((ttot, hd), F32), pltpu.VMEM((ttot, hd), F32),
                        pltpu.VMEM((ttot, 2 * hd), F32),
                        pltpu.VMEM((ttot, 128), F32),
                        pltpu.VMEM((4, ttot, hd), F32),
                        pltpu.VMEM((4, ttot, hd), BF16),
                        pltpu.VMEM((4, ttot, hd), BF16),
                        pltpu.VMEM((4, ttot, hd), BF16),
                        pltpu.VMEM((4, ttot, GDN_CHUNK), BF16),
                        pltpu.VMEM((4, ttot // GDN_CHUNK, hd), F32),
                        pltpu.VMEM((4, ttot, hd), F32),
                        pltpu.VMEM((4, hd, hd), F32)],
        compiler_params=_cparams(("parallel", "parallel")),
        name="gdn_mixer",
    )(gtc, gtl, zc, zc, zc, zc, zl, zl, zl, zl, conv_w, conv_w, conv_w, norm_w.reshape(1, hd))
    return al, ac


def gdn_gate_weights(w_ab, a_log, dt_bias):
    nv = a_log.shape[-1]
    nqk = nv // 2
    cols, alog, dtb = [], [], []
    for j in range(nqk):
        for kind in range(2):
            for d in range(2):
                for r in range(2):
                    hv = 2 * j + r
                    cols.append(w_ab[d][:, kind * nv + hv])
                    alog.append(a_log[d, hv] if kind == 0 else jnp.zeros((), F32))
                    dtb.append(dt_bias[d, hv] if kind == 0 else jnp.zeros((), F32))
    return jnp.stack(cols, axis=1), jnp.stack(alog)[:, None], jnp.stack(dtb)[:, None]


def _na_cases(rows):
    kh = min(WIN_H, rows)
    nbk = min(NA_KROWS, rows)
    cases, case_of_block, kstart = [], [], []
    for i in range(rows // NA_QROWS):
        r0 = i * NA_QROWS
        b0 = int(np.clip(r0 - kh // 2, 0, rows - nbk))
        sig = []
        for qr in range(NA_QROWS):
            q_row = r0 + qr
            row_start = int(np.clip(q_row - kh // 2, 0, rows - kh))
            for kr in range(nbk):
                k_row = b0 + kr
                ok = row_start <= k_row < row_start + kh
                dr = int(np.clip(k_row - q_row + WIN_H - 1, 0, 2 * WIN_H - 2))
                sig.append((ok, dr))
        sig = tuple(sig)
        if sig not in cases:
            cases.append(sig)
        case_of_block.append(cases.index(sig))
        kstart.append(b0 * GRID_W)
    return cases, case_of_block, kstart, nbk


def _na_bias_kernel(rpb_ref, o_ref, *, cases, nbk):
    h = pl.program_id(0)
    ndr, ndc = 2 * WIN_H - 1, 2 * WIN_W - 1
    kwid = min(WIN_W, GRID_W)
    qc = lax.broadcasted_iota(jnp.int32, (GRID_W, 128), 0)
    lane = lax.broadcasted_iota(jnp.int32, (GRID_W, 128), 1)
    kc = lane % GRID_W
    second = (lane // GRID_W).astype(F32)
    col_start = jnp.clip(qc - kwid // 2, 0, GRID_W - kwid)
    col_ok = (kc >= col_start) & (kc < col_start + kwid)
    dc = jnp.clip(kc - qc + WIN_W - 1, 0, ndc - 1)
    base = h * (ndr * ndc)
    cache = {}

    def tile_for(dr0, dr1):
        key = (dr0, dr1)
        if key not in cache:
            val = jnp.zeros((GRID_W, 128), F32)
            for n in range(ndc):
                s0 = rpb_ref[base + dr0 * ndc + n]
                s1 = rpb_ref[base + dr1 * ndc + n]
                val = jnp.where(dc == n, s0 + (s1 - s0) * second, val)
            cache[key] = val
        return cache[key]

    for ci, sig in enumerate(cases):
        for qr in range(NA_QROWS):
            for lt in range(nbk // 2):
                ok0, dr0 = sig[qr * nbk + 2 * lt]
                ok1, dr1 = sig[qr * nbk + 2 * lt + 1]
                if not (ok0 or ok1):
                    tile = jnp.full((GRID_W, 128), NEG_BIG, F32)
                else:
                    row_pen = (float(ok0) - 1.0) + (float(ok1) - float(ok0)) * second
                    tile = jnp.where(col_ok, tile_for(dr0, dr1), NEG_BIG) - row_pen * NEG_BIG
                o_ref[0, ci, qr * GRID_W:(qr + 1) * GRID_W, lt * 128:(lt + 1) * 128] = tile


def na_bias_tables(rpb, cases, nbk):
    nh = rpb.shape[0]
    ncase = len(cases)
    qblk = NA_QROWS * GRID_W
    return pl.pallas_call(
        functools.partial(_na_bias_kernel, cases=cases, nbk=nbk),
        grid=(nh,),
        in_specs=[pl.BlockSpec(memory_space=pltpu.SMEM)],
        out_specs=pl.BlockSpec((1, ncase, qblk, nbk * GRID_W), lambda h: (h, 0, 0, 0)),
        out_shape=jax.ShapeDtypeStruct((nh, ncase, qblk, nbk * GRID_W), F32),
        compiler_params=_cparams(("arbitrary",)),
        name="na_bias_tables",
    )(rpb.reshape(-1))


def _na_kernel(case_ref, kstart_ref, q_ref, k_ref, v_ref, qc_ref, kc_ref, vc_ref, bias_ref,
               ol_ref, oc_ref, *, nloc, scale, group):
    i = pl.program_id(2)
    kc = kc_ref[...]
    vc = vc_ref[...]

    @pl.when(i == 0)
    def _():
        s = _dot_nt(qc_ref[...], kc) * scale
        p = jnp.exp(s - jnp.max(s, axis=-1, keepdims=True))
        o = _dot(p.astype(BF16), vc) / jnp.sum(p, axis=-1, keepdims=True)
        oc_ref[...] = o.astype(BF16)

    gs = range(group)
    qblk = q_ref.shape[0] // group
    q = [q_ref[g * qblk:(g + 1) * qblk, :] for g in gs]
    ks = [pl.multiple_of(kstart_ref[i * group + g], GRID_W) for g in gs]
    s_loc = [_dot_nt(q[g], k_ref[pl.ds(ks[g], nloc), :]) * scale + bias_ref[0, case_ref[i * group + g]]
             for g in gs]
    s_ctx = [_dot_nt(q[g], kc) * scale for g in gs]
    m = [jnp.maximum(jnp.max(s_loc[g], axis=-1, keepdims=True), jnp.max(s_ctx[g], axis=-1, keepdims=True))
         for g in gs]
    p_loc = [jnp.exp(s_loc[g] - m[g]) for g in gs]
    p_ctx = [jnp.exp(s_ctx[g] - m[g]) for g in gs]
    denom = [jnp.sum(p_loc[g], axis=-1, keepdims=True) + jnp.sum(p_ctx[g], axis=-1, keepdims=True)
             for g in gs]
    pv = [_dot(p_loc[g].astype(BF16), v_ref[pl.ds(ks[g], nloc), :]) + _dot(p_ctx[g].astype(BF16), vc)
          for g in gs]
    for g in gs:
        ol_ref[g * qblk:(g + 1) * qblk, :] = (pv[g] / denom[g]).astype(BF16)


def na_mixer(zc, zl, batch, rpb):
    tc = zc.shape[0] // batch
    tl = zl.shape[0] // batch
    d = zc.shape[1] // 3
    nh = NA_HEADS
    hd = d // nh
    rows = tl // GRID_W
    cases, case_of_block, kstart, nbk = _na_cases(rows)
    bias = na_bias_tables(rpb, cases, nbk)
    qblk = NA_QROWS * GRID_W
    nblk = rows // NA_QROWS
    group = NA_GROUP if nblk % NA_GROUP == 0 else 1
    nstep = nblk // group
    nloc = nbk * GRID_W
    grid_spec = pltpu.PrefetchScalarGridSpec(
        num_scalar_prefetch=2,
        grid=(nh, batch, nstep),
        in_specs=[pl.BlockSpec((group * qblk, hd), lambda h, b, i, *_: (b * nstep + i, h)),
                  pl.BlockSpec((tl, hd), lambda h, b, i, *_: (b, nh + h)),
                  pl.BlockSpec((tl, hd), lambda h, b, i, *_: (b, 2 * nh + h)),
                  pl.BlockSpec((tc, hd), lambda h, b, i, *_: (b, h)),
                  pl.BlockSpec((tc, hd), lambda h, b, i, *_: (b, nh + h)),
                  pl.BlockSpec((tc, hd), lambda h, b, i, *_: (b, 2 * nh + h)),
                  pl.BlockSpec((1, len(cases), qblk, nloc), lambda h, b, i, *_: (h, 0, 0, 0))],
        out_specs=[pl.BlockSpec((group * qblk, hd), lambda h, b, i, *_: (b * nstep + i, h)),
                   pl.BlockSpec((tc, hd), lambda h, b, i, *_: (b, h))])
    ol, oc = pl.pallas_call(
        functools.partial(_na_kernel, nloc=nloc, scale=hd ** -0.5, group=group),
        grid_spec=grid_spec,
        out_shape=[jax.ShapeDtypeStruct((batch * tl, d), BF16),
                   jax.ShapeDtypeStruct((batch * tc, d), BF16)],
        compiler_params=_cparams(("parallel", "parallel", "arbitrary")),
        name="na_mixer",
    )(jnp.asarray(case_of_block, jnp.int32), jnp.asarray(kstart, jnp.int32),
      zl, zl, zl, zc, zc, zc, bias)
    return ol, oc


TM = 512
TM_PROJ = 1024
TN_PROJ = 1024
TN_OUT = 512
TF_MLP = 1024
TN_MLP = 256


def kernel(x, c, ctx, c_ctx, ada_w, ada_b, norm_g, mlp_w1, mlp_w2, ret_w_in, ret_w_out, gdn_w_in, gdn_conv_w, gdn_w_ab, gdn_a_log, gdn_dt_bias, gdn_norm_w, gdn_w_out, na_w_in, na_w_out, na_rpb):
    batch, t, d = x.shape
    tctx = ctx.shape[1]
    depth = ada_w.shape[0]

    nrow = -(-(batch + 1) // 16) * 16
    cvec = jnp.zeros((nrow, d), F32).at[0].set(c_ctx).at[1:batch + 1].set(c)
    mods = ada_modulation(cvec, ada_w, ada_b).reshape(depth, nrow, N_MOD, d)

    xl = x.reshape(batch * t, d)
    xc = ctx.reshape(batch * tctx, d)
    ret_lg, ret_cos, ret_sin = retention_tables(t, ret_w_in.shape[2] // 6 // RET_HEADS)

    i_ret = i_gdn = i_na = 0
    for i in range(depth):
        last = i == depth - 1
        m_ctx = mods[i, 0:1]
        m_lat = mods[i, 1:batch + 1]
        g = norm_g[i]
        kind = i % 3

        def project(w, out_dtype=BF16, tn=TN_PROJ):
            ws = column_slabs(w, min(tn, w.shape[1]))
            zc = norm_mod_matmul(xc, m_ctx, batch * tctx, g[0], ws, 0, 1, out_dtype, TM_PROJ)
            zl = norm_mod_matmul(xl, m_lat, t, g[0], ws, 0, 1, out_dtype, TM_PROJ)
            return zc, zl

        if kind == 0:
            zc, zl = project(ret_w_in[i_ret].astype(BF16))
            al, ac = retention_mixer(zc, zl, batch, ret_lg, ret_cos, ret_sin, not last)
            w_out = ret_w_out[i_ret].astype(BF16)
            i_ret += 1
        elif kind == 1:
            zc, zl = project(gdn_w_in[i_gdn].astype(BF16))
            w_gate, alog_col, dtb_col = gdn_gate_weights(gdn_w_ab[i_gdn], gdn_a_log[i_gdn],
                                                        gdn_dt_bias[i_gdn])
            abc, abl = project(w_gate.astype(BF16), out_dtype=F32)
            gtc = gdn_gates(abc.reshape(batch, tctx, -1).transpose(0, 2, 1), alog_col, dtb_col)
            gtl = gdn_gates(abl.reshape(batch, t, -1).transpose(0, 2, 1), alog_col, dtb_col)
            al, ac = gdn_mixer(zc, zl, gtc, gtl, batch, gdn_conv_w[i_gdn], gdn_norm_w[i_gdn])
            w_out = gdn_w_out[i_gdn].astype(BF16)
            i_gdn += 1
        else:
            zc, zl = project(na_w_in[i_na].astype(BF16))
            al, ac = na_mixer(zc, zl, batch, na_rpb[i_na])
            w_out = na_w_out[i_na].astype(BF16)
            i_na += 1

        w1 = column_slabs(mlp_w1[i].astype(BF16), TF_MLP)
        w2 = column_slabs(mlp_w2[i].astype(BF16), TN_MLP)
        w_out = column_slabs(w_out, TN_OUT)
        xl = out_proj_residual(al, w_out, xl, m_lat, t, g[1], 2, TM)
        xl = mlp_residual(xl, m_lat, t, g[2], g[3], w1, w2, TM)
        if not last:
            xc = out_proj_residual(ac, w_out, xc, m_ctx, batch * tctx, g[1], 2, TM)
            xc = mlp_residual(xc, m_ctx, batch * tctx, g[2], g[3], w1, w2, TM)
    return xl.reshape(batch, t, d)
```

```python
import functools
import math

import numpy as np
import jax
import jax.numpy as jnp
from jax import lax
from jax.experimental import pallas as pl
from jax.experimental.pallas import tpu as pltpu

F32 = jnp.float32
BF16 = jnp.bfloat16

NORM_EPS = 1e-6
L2_EPS = 1e-6
ROPE_BASE = 10000.0
N_MOD = 6
GRID_W = 64

RET_HEADS = 8
RET_CHUNK = 256
GDN_HEAD_DIM = 128
GDN_CHUNK = 128
GDN_CONV = 5
NA_HEADS = 16
WIN_H = 8
WIN_W = 16
NA_QROWS = 2
NA_KROWS = 10
NEG_BIG = -1e30
NA_GROUP = 4

V7X_VMEM_LIMIT_BYTES = 56 * 1024 * 1024


def _cparams(sem):
    return pltpu.CompilerParams(dimension_semantics=sem, vmem_limit_bytes=V7X_VMEM_LIMIT_BYTES)


def _sigmoid(x):
    return 1.0 / (1.0 + jnp.exp(-x))


def _silu(x):
    return x * _sigmoid(x)


def _dot(a, b):
    return jnp.dot(a, b, preferred_element_type=F32)


def _dot_nt(a, b):
    return lax.dot_general(a, b, (((1,), (1,)), ((), ())), preferred_element_type=F32)


def _dot_tn(a, b):
    return lax.dot_general(a, b, (((0,), (0,)), ((), ())), preferred_element_type=F32)


def _rms(y, g):
    return y * lax.rsqrt(jnp.mean(y * y, axis=-1, keepdims=True) + NORM_EPS) * g


def _ada_kernel(c_ref, w_ref, b_ref, o_ref):
    s = _silu(c_ref[...]).astype(BF16)
    o_ref[0] = _dot(s, w_ref[0].astype(BF16)) + b_ref[0]


def ada_modulation(cvec, ada_w, ada_b, tn=1024):
    depth, d, n = ada_w.shape
    r = cvec.shape[0]
    return pl.pallas_call(
        _ada_kernel,
        grid=(depth, n // tn),
        in_specs=[pl.BlockSpec((r, d), lambda l, j: (0, 0)),
                  pl.BlockSpec((1, d, tn), lambda l, j: (l, 0, j)),
                  pl.BlockSpec((1, 1, tn), lambda l, j: (l, 0, j))],
        out_specs=pl.BlockSpec((1, r, tn), lambda l, j: (l, 0, j)),
        out_shape=jax.ShapeDtypeStruct((depth, r, n), F32),
        compiler_params=_cparams(("parallel", "parallel")),
        name="ada_modulation",
    )(cvec, ada_w, ada_b.reshape(depth, 1, n))


def _modulated_norm(x, g, m_ref, shift_idx, scale_idx):
    y = _rms(x, g)
    return y * (1.0 + m_ref[0, scale_idx:scale_idx + 1, :]) + m_ref[0, shift_idx:shift_idx + 1, :]


def _nmm_kernel(x_ref, m_ref, g_ref, w_ref, o_ref, h_ref, *, shift_idx, scale_idx):
    @pl.when(pl.program_id(1) == 0)
    def _():
        h_ref[...] = _modulated_norm(x_ref[...], g_ref[...], m_ref, shift_idx, scale_idx).astype(BF16)

    o_ref[...] = _dot(h_ref[...], w_ref[0]).astype(o_ref.dtype)


def norm_mod_matmul(x, mods, tokens_per_mod, g, w_slabs, shift_idx, scale_idx, out_dtype, tm):
    n, d = x.shape
    nj, _, tn = w_slabs.shape
    nout = nj * tn
    w = w_slabs
    tm = min(tm, tokens_per_mod)
    bpm = tokens_per_mod // tm
    return pl.pallas_call(
        functools.partial(_nmm_kernel, shift_idx=shift_idx, scale_idx=scale_idx),
        grid=(n // tm, nout // tn),
        in_specs=[pl.BlockSpec((tm, d), lambda i, j: (i, 0)),
                  pl.BlockSpec((1, N_MOD, d), lambda i, j: (i // bpm, 0, 0)),
                  pl.BlockSpec((1, d), lambda i, j: (0, 0)),
                  pl.BlockSpec((1, d, tn), lambda i, j: (j, 0, 0))],
        out_specs=pl.BlockSpec((tm, tn), lambda i, j: (i, j)),
        out_shape=jax.ShapeDtypeStruct((n, nout), out_dtype),
        scratch_shapes=[pltpu.VMEM((tm, d), BF16)],
        compiler_params=_cparams(("parallel", "arbitrary")),
        name="norm_mod_matmul",
    )(x, mods, g.reshape(1, d), w)


def _rms_residual_slabs(y_ref, x_ref, gate, g_ref, o_ref):
    ns, _, tn = y_ref.shape
    ss = jnp.sum(y_ref[0] * y_ref[0], axis=-1, keepdims=True)
    for j in range(1, ns):
        ss = ss + jnp.sum(y_ref[j] * y_ref[j], axis=-1, keepdims=True)
    inv = lax.rsqrt(ss / (ns * tn) + NORM_EPS)
    for j in range(ns):
        sl = slice(j * tn, (j + 1) * tn)
        o_ref[:, sl] = x_ref[:, sl] + gate[:, sl] * (y_ref[j] * inv * g_ref[:, sl])


def _out_res_kernel(a_ref, w_ref, x_ref, m_ref, g_ref, o_ref, y_ref, *, gate_idx, nj):
    for j in range(nj):
        y_ref[j] = _dot(a_ref[...], w_ref[j])
    _rms_residual_slabs(y_ref, x_ref, m_ref[0, gate_idx:gate_idx + 1, :], g_ref, o_ref)


def column_slabs(w, tn):
    k, n = w.shape
    return w.reshape(k, n // tn, tn).transpose(1, 0, 2)


def out_proj_residual(a, w_slabs, x, mods, tokens_per_mod, g, gate_idx, tm):
    n, kdim = a.shape
    nj, _, tn = w_slabs.shape
    d = nj * tn
    tm = min(tm, tokens_per_mod)
    bpm = tokens_per_mod // tm
    w = w_slabs
    return pl.pallas_call(
        functools.partial(_out_res_kernel, gate_idx=gate_idx, nj=nj),
        grid=(n // tm,),
        in_specs=[pl.BlockSpec((tm, kdim), lambda i: (i, 0)),
                  pl.BlockSpec((nj, kdim, tn), lambda i: (0, 0, 0), pipeline_mode=pl.Buffered(1)),
                  pl.BlockSpec((tm, d), lambda i: (i, 0)),
                  pl.BlockSpec((1, N_MOD, d), lambda i: (i // bpm, 0, 0)),
                  pl.BlockSpec((1, d), lambda i: (0, 0))],
        out_specs=pl.BlockSpec((tm, d), lambda i: (i, 0)),
        out_shape=jax.ShapeDtypeStruct((n, d), F32),
        scratch_shapes=[pltpu.VMEM((nj, tm, tn), F32)],
        compiler_params=_cparams(("parallel",)),
        name="out_proj_residual",
    )(a, w, x, mods, g.reshape(1, d))


def _mlp_kernel(x_ref, m_ref, g2_ref, g3_ref, w1_ref, w2_ref, o_ref, h_ref, acc_ref, *, nk):
    k = pl.program_id(1)

    @pl.when(k == 0)
    def _():
        h_ref[...] = _modulated_norm(x_ref[...], g2_ref[...], m_ref, 3, 4).astype(BF16)
        acc_ref[...] = jnp.zeros_like(acc_ref)

    a = jnp.maximum(_dot(h_ref[...], w1_ref[0]), 0.0)
    acc_ref[...] += _dot((a * a).astype(BF16), w2_ref[...])

    @pl.when(k == nk - 1)
    def _():
        yn = _rms(acc_ref[...], g3_ref[...])
        o_ref[...] = x_ref[...] + m_ref[0, 5:6, :] * yn


def mlp_residual(x, mods, tokens_per_mod, g2, g3, w1_slabs, w2, tm):
    n, d = x.shape
    nk, _, tf = w1_slabs.shape
    tm = min(tm, tokens_per_mod)
    bpm = tokens_per_mod // tm
    return pl.pallas_call(
        functools.partial(_mlp_kernel, nk=nk),
        grid=(n // tm, nk),
        in_specs=[pl.BlockSpec((tm, d), lambda i, k: (i, 0)),
                  pl.BlockSpec((1, N_MOD, d), lambda i, k: (i // bpm, 0, 0)),
                  pl.BlockSpec((1, d), lambda i, k: (0, 0)),
                  pl.BlockSpec((1, d), lambda i, k: (0, 0)),
                  pl.BlockSpec((1, d, tf), lambda i, k: (k, 0, 0)),
                  pl.BlockSpec((tf, d), lambda i, k: (k, 0))],
        out_specs=pl.BlockSpec((tm, d), lambda i, k: (i, 0)),
        out_shape=jax.ShapeDtypeStruct((n, d), F32),
        scratch_shapes=[pltpu.VMEM((tm, d), BF16), pltpu.VMEM((tm, d), F32)],
        compiler_params=_cparams(("parallel", "arbitrary")),
        name="mlp_residual",
    )(x, mods, g2.reshape(1, d), g3.reshape(1, d), w1_slabs, w2)


def _ret_kernel(lg_ref, cos_ref, sin_ref, qc_ref, kc_ref, vc_ref, gc_ref, ql_ref, kl_ref, vl_ref, gl_ref,
                *rest, tc, tl, dk, dv, with_ctx_out):
    if with_ctx_out:
        ac_ref, al_ref, qs_ref, ks_ref, of_ref, ob_ref, rf_ref, rb_ref = rest
    else:
        al_ref, qs_ref, ks_ref, of_ref, ob_ref, rf_ref, rb_ref = rest
        ac_ref = None
    c = RET_CHUNK
    half = dk // 2
    k_scale = dk ** -0.5

    qs_ref[0:tc, :] = qc_ref[...]
    ks_ref[0:tc, :] = (kc_ref[...].astype(F32) * k_scale).astype(BF16)

    rt = 256

    def rope_body(i, carry):
        r0 = pl.multiple_of(i * rt, rt)
        cs = cos_ref[pl.ds(r0, rt), :]
        sn = sin_ref[pl.ds(r0, rt), :]
        for src, dst, scale in ((ql_ref, qs_ref, 1.0), (kl_ref, ks_ref, k_scale)):
            t = src[pl.ds(r0, rt), :].astype(F32) * scale
            t1, t2 = t[:, :half], t[:, half:]
            dst[pl.ds(tc + r0, rt), :] = jnp.concatenate(
                [t1 * cs - t2 * sn, t1 * sn + t2 * cs], axis=-1).astype(BF16)
        return carry

    lax.fori_loop(0, tl // rt, rope_body, 0)

    lgf = lg_ref[0, 0:1, 0:1]
    lgb = lg_ref[0, 1:2, 0:1]
    ci = lax.broadcasted_iota(jnp.int32, (c, c), 0)
    mi = lax.broadcasted_iota(jnp.int32, (c, c), 1)
    dist = (ci - mi).astype(F32)
    intra_f = jnp.where(dist >= 0, jnp.exp(lgf * jnp.maximum(dist, 0.0)), 0.0)
    intra_b = jnp.where(dist <= 0, jnp.exp(lgb * jnp.maximum(-dist, 0.0)), 0.0)
    pos = lax.broadcasted_iota(jnp.int32, (c, 1), 0).astype(F32)
    xi_f = jnp.exp(lgf * (pos + 1.0))
    zeta_f = jnp.exp(lgf * (c - 1.0 - pos))
    xi_b = jnp.exp(lgb * (c - pos))
    zeta_b = jnp.exp(lgb * pos)
    cd_f = jnp.exp(lgf * float(c))
    cd_b = jnp.exp(lgb * float(c))

    rf_ref[...] = jnp.zeros_like(rf_ref)
    rb_ref[...] = jnp.zeros_like(rb_ref)
    dirs = ((rf_ref, of_ref, intra_f, xi_f, zeta_f, cd_f), (rb_ref, ob_ref, intra_b, xi_b, zeta_b, cd_b))

    def make_body(v_ref, base, n):
        def body(i, carry):
            loc = (pl.multiple_of(i * c, c), pl.multiple_of((n - 1 - i) * c, c))
            qt = [qs_ref[pl.ds(base + l, c), :] for l in loc]
            kt = [ks_ref[pl.ds(base + l, c), :] for l in loc]
            vt = [v_ref[pl.ds(l, c), :] for l in loc]
            r = [dr[0][...] for dr in dirs]
            s = [_dot_nt(q, k) for q, k in zip(qt, kt)]
            qr = [_dot(q, x.astype(BF16)) for q, x in zip(qt, r)]
            kv = [_dot_tn(k, (v.astype(F32) * dr[4]).astype(BF16)) for k, v, dr in zip(kt, vt, dirs)]
            sv = [_dot((x * dr[2]).astype(BF16), v) for x, v, dr in zip(s, vt, dirs)]
            for d, dr in enumerate(dirs):
                dr[1][pl.ds(base + loc[d], c), :] = sv[d] + qr[d] * dr[3]
                dr[0][...] = r[d] * dr[5] + kv[d]
            return carry
        return body

    lax.fori_loop(0, tc // c, make_body(vc_ref, 0, tc // c), 0)
    lax.fori_loop(0, tl // c, make_body(vl_ref, tc, tl // c), 0)

    def finish(g_ref, a_ref, base, n):
        def body(i, carry):
            r0 = pl.multiple_of(i * rt, rt)
            y = of_ref[pl.ds(base + r0, rt), :] + ob_ref[pl.ds(base + r0, rt), :]
            yn = y * lax.rsqrt(jnp.mean(y * y, axis=-1, keepdims=True) + NORM_EPS)
            g = g_ref[pl.ds(r0, rt), :].astype(F32)
            a_ref[pl.ds(r0, rt), :] = (_silu(g) * yn).astype(BF16)
            return carry
        lax.fori_loop(0, n // rt, body, 0)

    if with_ctx_out:
        finish(gc_ref, ac_ref, 0, tc)
    finish(gl_ref, al_ref, tc, tl)


def retention_mixer(zc, zl, batch, lg, cos, sin, with_ctx_out):
    tc = zc.shape[0] // batch
    tl = zl.shape[0] // batch
    width = zc.shape[1]
    dk = width // 6 // RET_HEADS
    dv = 2 * dk
    h = RET_HEADS
    kern = functools.partial(_ret_kernel, tc=tc, tl=tl, dk=dk, dv=dv, with_ctx_out=with_ctx_out)

    def seg_specs(t):
        return [pl.BlockSpec((t, dk), lambda b, j: (b, j)),
                pl.BlockSpec((t, dk), lambda b, j: (b, h + j)),
                pl.BlockSpec((t, dv), lambda b, j: (b, h + j)),
                pl.BlockSpec((t, dv), lambda b, j: (b, 2 * h + j))]

    out_shape = [jax.ShapeDtypeStruct((batch * tl, h * dv), BF16)]
    out_specs = [pl.BlockSpec((tl, dv), lambda b, j: (b, j))]
    if with_ctx_out:
        out_shape = [jax.ShapeDtypeStruct((batch * tc, h * dv), BF16)] + out_shape
        out_specs = [pl.BlockSpec((tc, dv), lambda b, j: (b, j))] + out_specs
    outs = pl.pallas_call(
        kern,
        grid=(batch, h),
        in_specs=[pl.BlockSpec((1, 8, 128), lambda b, j: (j, 0, 0)),
                  pl.BlockSpec((tl, dk // 2), lambda b, j: (0, 0)),
                  pl.BlockSpec((tl, dk // 2), lambda b, j: (0, 0))] + seg_specs(tc) + seg_specs(tl),
        out_specs=out_specs,
        out_shape=out_shape,
        scratch_shapes=[pltpu.VMEM((tc + tl, dk), BF16), pltpu.VMEM((tc + tl, dk), BF16),
                        pltpu.VMEM((tc + tl, dv), F32), pltpu.VMEM((tc + tl, dv), F32),
                        pltpu.VMEM((dk, dv), F32), pltpu.VMEM((dk, dv), F32)],
        compiler_params=_cparams(("parallel", "parallel")),
        name="retention_mixer",
    )(lg, cos, sin, zc, zc, zc, zc, zl, zl, zl, zl)
    if with_ctx_out:
        return outs[1], outs[0]
    return outs[0], None


def retention_tables(t, dk):
    fwd = jnp.log(1.0 - 2.0 ** (-5.0 - jnp.arange(RET_HEADS, dtype=F32)))
    lg = jnp.zeros((RET_HEADS, 8, 128), F32)
    lg = lg.at[:, 0, :].set(fwd[:, None]).at[:, 1, :].set(fwd[::-1][:, None])
    tt = jnp.arange(t)
    row = (tt // GRID_W).astype(F32)
    col = (tt % GRID_W).astype(F32)
    n_pairs = dk // 2
    inv = ROPE_BASE ** (-jnp.arange(0, n_pairs, 2, dtype=F32) / n_pairs)
    ang = jnp.concatenate([row[:, None] * inv, col[:, None] * inv], axis=-1)
    return lg, jnp.cos(ang), jnp.sin(ang)


def _split3(x):
    hi = x.astype(BF16)
    r1 = x - hi.astype(F32)
    mid = r1.astype(BF16)
    lo = (r1 - mid.astype(F32)).astype(BF16)
    return hi, mid, lo


def _gdn_gates_kernel(ab_ref, alog_ref, dtb_ref, o_ref, *, t):
    c = GDN_CHUNK
    nrow = ab_ref.shape[1]
    rowid = lax.broadcasted_iota(jnp.int32, (nrow, 1), 0) % 8
    mi = lax.broadcasted_iota(jnp.int32, (c, c), 0)
    ci = lax.broadcasted_iota(jnp.int32, (c, c), 1)
    pre = (mi <= ci).astype(BF16)
    suf = (mi >= ci).astype(BF16)
    a_neg = -jnp.exp(alog_ref[...])
    dtb = dtb_ref[...]
    for w in range(t // c):
        x = ab_ref[0, :, w * c:(w + 1) * c]
        z = x + dtb
        g = a_neg * (jnp.maximum(z, 0.0) + jnp.log1p(jnp.exp(-jnp.abs(z))))
        parts = _split3(g)
        cp = sum(_dot(p, pre) for p in parts)
        cs = sum(_dot(p, suf) for p in parts)
        o_ref[0, :, w * c:(w + 1) * c] = jnp.where(rowid < 2, cp, jnp.where(rowid < 4, cs, _sigmoid(x)))


def gdn_gates(ab_t, alog_col, dtb_col):
    b, nrow, t = ab_t.shape
    return pl.pallas_call(
        functools.partial(_gdn_gates_kernel, t=t),
        grid=(b,),
        in_specs=[pl.BlockSpec((1, nrow, t), lambda i: (i, 0, 0)),
                  pl.BlockSpec((nrow, 1), lambda i: (0, 0)),
                  pl.BlockSpec((nrow, 1), lambda i: (0, 0))],
        out_specs=pl.BlockSpec((1, nrow, t), lambda i: (i, 0, 0)),
        out_shape=jax.ShapeDtypeStruct((b, nrow, t), F32),
        compiler_params=_cparams(("parallel",)),
        name="gdn_gates",
    )(ab_t, alog_col, dtb_col)


def _dot_each(a_list, b_list):
    return [_dot(a.astype(BF16), b.astype(BF16)) for a, b in zip(a_list, b_list)]


GDN_INV_BASE = 16
GDN_PREP_CHUNKS = 2


def _unit_triangular_inverse_minus_eye(ms, ri, li):
    n = ms[0].shape[0]
    s = GDN_INV_BASE
    inblk = (ri // s) == (li // s)
    ps = [jnp.where(inblk, -m, 0.0) for m in ms]
    qs = ps
    ps = _dot_each(ps, ps)
    level = 2
    while 2 * level < s:
        both = _dot_each([jnp.concatenate([q, p], axis=0) for q, p in zip(qs, ps)], ps)
        qs = [q + p + b[:n] for q, p, b in zip(qs, ps, both)]
        ps = [b[n:] for b in both]
        level *= 2
    qp = _dot_each(qs, ps)
    eye = jnp.where(ri == li, 1.0, 0.0)
    xs = [eye + q + p + t for q, p, t in zip(qs, ps, qp)]
    while s < n:
        off = ((ri // (2 * s)) == (li // (2 * s))) & ((ri // s) != (li // s))
        cxs = _dot_each([jnp.where(off, m, 0.0) for m in ms], xs)
        xcx = _dot_each(xs, cxs)
        xs = [x - t for x, t in zip(xs, xcx)]
        s *= 2
    return [x - eye for x in xs]


def _gdn_kernel(gtc_ref, gtl_ref, zqc_ref, zkc_ref, zvc_ref, zgc_ref, zql_ref, zkl_ref, zvl_ref, zgl_ref,
                cwq_ref, cwk_ref, cwv_ref, nw_ref, ac_ref, al_ref,
                pad_ref, q_s, k_s, v_s, gcol_s, u_s, w_s, qg_s, kd_s, a_s, egl_s, o_s, st_s, *, tc, tl):
    c = GDN_CHUNK
    hd = GDN_HEAD_DIM
    ttot = tc + tl
    rt = 256
    halo = 8
    kw = GDN_CONV
    lead = halo - (kw - 1) // 2

    def conv_seg(z_ref, cw_ref, n, width, dst_ref, base, l2_scale):
        zero = jnp.zeros((halo, width), F32)
        pad_ref[0:halo, 0:width] = zero
        pad_ref[halo + n:2 * halo + n, 0:width] = zero

        def fill(i, carry):
            r0 = pl.multiple_of(i * rt, rt)
            pad_ref[pl.ds(halo + r0, rt), 0:width] = z_ref[pl.ds(r0, rt), :].astype(F32)
            return carry
        lax.fori_loop(0, n // rt, fill, 0)
        cw = cw_ref[...]

        def body(i, carry):
            r0 = pl.multiple_of(i * rt, rt)
            xx = pad_ref[pl.ds(r0, rt + 2 * halo), 0:width]
            acc = xx[lead:lead + rt, :] * cw[0:1, :]
            for k in range(1, kw):
                acc = acc + xx[lead + k:lead + k + rt, :] * cw[k:k + 1, :]
            y = _silu(acc)
            if l2_scale is not None:
                for j in range(width // hd):
                    yj = y[:, j * hd:(j + 1) * hd]
                    yj = yj * lax.rsqrt(jnp.sum(yj * yj, axis=-1, keepdims=True) + L2_EPS) * l2_scale
                    dst_ref[pl.ds(base + r0, rt), j * hd:(j + 1) * hd] = yj
            else:
                dst_ref[pl.ds(base + r0, rt), :] = y
            return carry
        lax.fori_loop(0, n // rt, body, 0)

    for z_ref, n, base in ((zqc_ref, tc, 0), (zql_ref, tl, tc)):
        conv_seg(z_ref, cwq_ref, n, hd, q_s, base, hd ** -0.5)
    for z_ref, n, base in ((zkc_ref, tc, 0), (zkl_ref, tl, tc)):
        conv_seg(z_ref, cwk_ref, n, hd, k_s, base, 1.0)
    for z_ref, n, base in ((zvc_ref, tc, 0), (zvl_ref, tl, tc)):
        conv_seg(z_ref, cwv_ref, n, 2 * hd, v_s, base, None)

    def gate_cols(gt_ref, n, base):
        def body(i, carry):
            r0 = pl.multiple_of(i * c, c)
            tile = gt_ref[0, :, pl.ds(r0, c)]
            gcol_s[pl.ds(base + r0, c), :] = jnp.concatenate([tile] * (c // 8), axis=0).T
            return carry
        lax.fori_loop(0, n // c, body, 0)

    gate_cols(gtc_ref, tc, 0)
    gate_cols(gtl_ref, tl, tc)

    ri = lax.broadcasted_iota(jnp.int32, (c, c), 0)
    li = lax.broadcasted_iota(jnp.int32, (c, c), 1)

    def prep(gt_ref, n, base, cpi):
        def body(i, carry):
            rows, chans, mmats, rhss = [], [], [], []
            for cc in range(cpi):
                r0 = pl.multiple_of((i * cpi + cc) * c, c)
                row = base + r0
                g8 = gt_ref[0, :, pl.ds(r0, c)]
                gc8 = gcol_s[pl.ds(row, c), :]
                kb = k_s[pl.ds(row, c), :]
                qb = q_s[pl.ds(row, c), :]
                kbb = kb.astype(BF16)
                kk = _dot_nt(kbb, kbb)
                qk = _dot_nt(qb.astype(BF16), kbb)
                for d in range(2):
                    incl = (ri >= li) if d == 0 else (ri <= li)
                    strict = (ri > li) if d == 0 else (ri < li)
                    for r in range(2):
                        ch = 2 * d + r
                        g_row = g8[ch:ch + 1, :]
                        g_b = jnp.broadcast_to(gc8[:, ch:ch + 1], (c, hd))
                        beta_b = jnp.broadcast_to(gc8[:, 4 + ch:5 + ch], (c, hd))
                        eg_b = jnp.exp(g_b)
                        decay = jnp.exp(jnp.where(incl, g_b - g_row, -jnp.inf))
                        mmats.append(jnp.where(strict, kk * decay, 0.0) * beta_b)
                        rows.append(row)
                        chans.append(ch)
                        qg_s[ch, pl.ds(row, c), :] = (qb * eg_b).astype(BF16)
                        a_s[ch, pl.ds(row, c), :] = (qk * decay).astype(BF16)
                        g_last = g_row[:, c - 1:c] if d == 0 else g_row[:, 0:1]
                        kd_s[ch, pl.ds(row, c), :] = (kb * jnp.exp(g_last - g_b)).astype(BF16)
                        egl_s[ch, pl.ds(base // c + i * cpi + cc, 1), :] = jnp.broadcast_to(
                            jnp.exp(g_last), (1, hd))
                        vb = v_s[pl.ds(row, c), r * hd:(r + 1) * hd] * beta_b
                        rhss.append(jnp.concatenate([vb, kb * (beta_b * eg_b)], axis=1))
            tm1s = _unit_triangular_inverse_minus_eye(mmats, ri, li)
            for row, ch, rhs, corr in zip(rows, chans, rhss, _dot_each(tm1s, rhss)):
                sol = rhs + corr
                u_s[ch, pl.ds(row, c), :] = sol[:, :hd]
                w_s[ch, pl.ds(row, c), :] = sol[:, hd:].astype(BF16)
            return carry
        lax.fori_loop(0, n // (c * cpi), body, 0)

    prep(gtc_ref, tc, 0, GDN_PREP_CHUNKS)
    prep(gtl_ref, tl, tc, GDN_PREP_CHUNKS)

    st_s[...] = jnp.zeros_like(st_s)

    def scan(n, base):
        def body(i, carry):
            chs = range(4)
            idx = [i if ch < 2 else n // c - 1 - i for ch in chs]
            rows = [pl.multiple_of(base + idx[ch] * c, c) for ch in chs]
            s = [st_s[ch] for ch in chs]
            sb = [x.astype(BF16) for x in s]
            ws = [_dot(w_s[ch, pl.ds(rows[ch], c), :], sb[ch]) for ch in chs]
            qs = [_dot(qg_s[ch, pl.ds(rows[ch], c), :], sb[ch]) for ch in chs]
            vnb = [(u_s[ch, pl.ds(rows[ch], c), :] - ws[ch]).astype(BF16) for ch in chs]
            av = [_dot(a_s[ch, pl.ds(rows[ch], c), :], vnb[ch]) for ch in chs]
            kv = [_dot_tn(kd_s[ch, pl.ds(rows[ch], c), :], vnb[ch]) for ch in chs]
            for ch in chs:
                o_s[ch, pl.ds(rows[ch], c), :] = qs[ch] + av[ch]
                st_s[ch] = s[ch] * egl_s[ch, pl.ds(base // c + idx[ch], 1), :] + kv[ch]
            return carry
        lax.fori_loop(0, n // c, body, 0)

    scan(tc, 0)
    scan(tl, tc)

    nw = nw_ref[...]

    def finish(zg_ref, a_ref, n, base):
        def body(i, carry):
            r0 = pl.multiple_of(i * rt, rt)
            for r in range(2):
                y = o_s[r, pl.ds(base + r0, rt), :] + o_s[2 + r, pl.ds(base + r0, rt), :]
                yn = y * lax.rsqrt(jnp.mean(y * y, axis=-1, keepdims=True) + NORM_EPS) * nw
                g = zg_ref[pl.ds(r0, rt), r * hd:(r + 1) * hd].astype(F32)
                a_ref[pl.ds(r0, rt), r * hd:(r + 1) * hd] = (yn * _silu(g)).astype(BF16)
            return carry
        lax.fori_loop(0, n // rt, body, 0)

    finish(zgc_ref, ac_ref, tc, 0)
    finish(zgl_ref, al_ref, tl, tc)


def gdn_mixer(zc, zl, gtc, gtl, batch, conv_w, norm_w):
    tc = zc.shape[0] // batch
    tl = zl.shape[0] // batch
    hd = GDN_HEAD_DIM
    nqk = zc.shape[1] // hd // 6
    ttot = tc + tl
    kern = functools.partial(_gdn_kernel, tc=tc, tl=tl)

    def seg_specs(t):
        return [pl.BlockSpec((t, hd), lambda b, j: (b, j)),
                pl.BlockSpec((t, hd), lambda b, j: (b, nqk + j)),
                pl.BlockSpec((t, 2 * hd), lambda b, j: (b, nqk + j)),
                pl.BlockSpec((t, 2 * hd), lambda b, j: (b, 2 * nqk + j))]

    kw = conv_w.shape[0]
    ac, al = pl.pallas_call(
        kern,
        grid=(batch, nqk),
        in_specs=[pl.BlockSpec((1, 8, tc), lambda b, j: (b, j, 0)),
                  pl.BlockSpec((1, 8, tl), lambda b, j: (b, j, 0))]
                 + seg_specs(tc) + seg_specs(tl)
                 + [pl.BlockSpec((kw, hd), lambda b, j: (0, j)),
                    pl.BlockSpec((kw, hd), lambda b, j: (0, nqk + j)),
                    pl.BlockSpec((kw, 2 * hd), lambda b, j: (0, nqk + j)),
                    pl.BlockSpec((1, hd), lambda b, j: (0, 0))],
        out_specs=[pl.BlockSpec((tc, 2 * hd), lambda b, j: (b, j)),
                   pl.BlockSpec((tl, 2 * hd), lambda b, j: (b, j))],
        out_shape=[jax.ShapeDtypeStruct((batch * tc, 2 * nqk * hd), BF16),
                   jax.ShapeDtypeStruct((batch * tl, 2 * nqk * hd), BF16)],
        scratch_shapes=[pltpu.VMEM((max(tc, tl) + 16, 2 * hd), F32),
                        pltpu.VMEM((ttot, hd), F32), pltpu.VMEM((ttot, hd), F32),
                        pltpu.VMEM((ttot, 2 * hd), F32),
                        pltpu.VMEM((ttot, 128), F32),
                        pltpu.VMEM((4, ttot, hd), F32),
                        pltpu.VMEM((4, ttot, hd), BF16),
                        pltpu.VMEM((4, ttot, hd), BF16),
                        pltpu.VMEM((4, ttot, hd), BF16),
                        pltpu.VMEM((4, ttot, GDN_CHUNK), BF16),
                        pltpu.VMEM((4, ttot // GDN_CHUNK, hd), F32),
                        pltpu.VMEM((4, ttot, hd), F32),
                        pltpu.VMEM((4, hd, hd), F32)],
        compiler_params=_cparams(("parallel", "parallel")),
        name="gdn_mixer",
    )(gtc, gtl, zc, zc, zc, zc, zl, zl, zl, zl, conv_w, conv_w, conv_w, norm_w.reshape(1, hd))
    return al, ac


def gdn_gate_weights(w_ab, a_log, dt_bias):
    nv = a_log.shape[-1]
    nqk = nv // 2
    cols, alog, dtb = [], [], []
    for j in range(nqk):
        for kind in range(2):
            for d in range(2):
                for r in range(2):
                    hv = 2 * j + r
                    cols.append(w_ab[d][:, kind * nv + hv])
                    alog.append(a_log[d, hv] if kind == 0 else jnp.zeros((), F32))
                    dtb.append(dt_bias[d, hv] if kind == 0 else jnp.zeros((), F32))
    return jnp.stack(cols, axis=1), jnp.stack(alog)[:, None], jnp.stack(dtb)[:, None]


def _na_cases(rows):
    kh = min(WIN_H, rows)
    nbk = min(NA_KROWS, rows)
    cases, case_of_block, kstart = [], [], []
    for i in range(rows // NA_QROWS):
        r0 = i * NA_QROWS
        b0 = int(np.clip(r0 - kh // 2, 0, rows - nbk))
        sig = []
        for qr in range(NA_QROWS):
            q_row = r0 + qr
            row_start = int(np.clip(q_row - kh // 2, 0, rows - kh))
            for kr in range(nbk):
                k_row = b0 + kr
                ok = row_start <= k_row < row_start + kh
                dr = int(np.clip(k_row - q_row + WIN_H - 1, 0, 2 * WIN_H - 2))
                sig.append((ok, dr))
        sig = tuple(sig)
        if sig not in cases:
            cases.append(sig)
        case_of_block.append(cases.index(sig))
        kstart.append(b0 * GRID_W)
    return cases, case_of_block, kstart, nbk


def _na_bias_kernel(rpb_ref, o_ref, *, cases, nbk):
    h = pl.program_id(0)
    ndr, ndc = 2 * WIN_H - 1, 2 * WIN_W - 1
    kwid = min(WIN_W, GRID_W)
    qc = lax.broadcasted_iota(jnp.int32, (GRID_W, 128), 0)
    lane = lax.broadcasted_iota(jnp.int32, (GRID_W, 128), 1)
    kc = lane % GRID_W
    second = (lane // GRID_W).astype(F32)
    col_start = jnp.clip(qc - kwid // 2, 0, GRID_W - kwid)
    col_ok = (kc >= col_start) & (kc < col_start + kwid)
    dc = jnp.clip(kc - qc + WIN_W - 1, 0, ndc - 1)
    base = h * (ndr * ndc)
    cache = {}

    def tile_for(dr0, dr1):
        key = (dr0, dr1)
        if key not in cache:
            val = jnp.zeros((GRID_W, 128), F32)
            for n in range(ndc):
                s0 = rpb_ref[base + dr0 * ndc + n]
                s1 = rpb_ref[base + dr1 * ndc + n]
                val = jnp.where(dc == n, s0 + (s1 - s0) * second, val)
            cache[key] = val
        return cache[key]

    for ci, sig in enumerate(cases):
        for qr in range(NA_QROWS):
            for lt in range(nbk // 2):
                ok0, dr0 = sig[qr * nbk + 2 * lt]
                ok1, dr1 = sig[qr * nbk + 2 * lt + 1]
                if not (ok0 or ok1):
                    tile = jnp.full((GRID_W, 128), NEG_BIG, F32)
                else:
                    row_pen = (float(ok0) - 1.0) + (float(ok1) - float(ok0)) * second
                    tile = jnp.where(col_ok, tile_for(dr0, dr1), NEG_BIG) - row_pen * NEG_BIG
                o_ref[0, ci, qr * GRID_W:(qr + 1) * GRID_W, lt * 128:(lt + 1) * 128] = tile


def na_bias_tables(rpb, cases, nbk):
    nh = rpb.shape[0]
    ncase = len(cases)
    qblk = NA_QROWS * GRID_W
    return pl.pallas_call(
        functools.partial(_na_bias_kernel, cases=cases, nbk=nbk),
        grid=(nh,),
        in_specs=[pl.BlockSpec(memory_space=pltpu.SMEM)],
        out_specs=pl.BlockSpec((1, ncase, qblk, nbk * GRID_W), lambda h: (h, 0, 0, 0)),
        out_shape=jax.ShapeDtypeStruct((nh, ncase, qblk, nbk * GRID_W), F32),
        compiler_params=_cparams(("arbitrary",)),
        name="na_bias_tables",
    )(rpb.reshape(-1))


def _na_kernel(case_ref, kstart_ref, q_ref, k_ref, v_ref, qc_ref, kc_ref, vc_ref, bias_ref,
               ol_ref, oc_ref, *, nloc, scale, group):
    i = pl.program_id(2)
    kc = kc_ref[...]
    vc = vc_ref[...]

    @pl.when(i == 0)
    def _():
        s = _dot_nt(qc_ref[...], kc) * scale
        p = jnp.exp(s - jnp.max(s, axis=-1, keepdims=True))
        o = _dot(p.astype(BF16), vc) / jnp.sum(p, axis=-1, keepdims=True)
        oc_ref[...] = o.astype(BF16)

    gs = range(group)
    qblk = q_ref.shape[0] // group
    q = [q_ref[g * qblk:(g + 1) * qblk, :] for g in gs]
    ks = [pl.multiple_of(kstart_ref[i * group + g], GRID_W) for g in gs]
    s_loc = [_dot_nt(q[g], k_ref[pl.ds(ks[g], nloc), :]) * scale + bias_ref[0, case_ref[i * group + g]]
             for g in gs]
    s_ctx = [_dot_nt(q[g], kc) * scale for g in gs]
    m = [jnp.maximum(jnp.max(s_loc[g], axis=-1, keepdims=True), jnp.max(s_ctx[g], axis=-1, keepdims=True))
         for g in gs]
    p_loc = [jnp.exp(s_loc[g] - m[g]) for g in gs]
    p_ctx = [jnp.exp(s_ctx[g] - m[g]) for g in gs]
    denom = [jnp.sum(p_loc[g], axis=-1, keepdims=True) + jnp.sum(p_ctx[g], axis=-1, keepdims=True)
             for g in gs]
    pv = [_dot(p_loc[g].astype(BF16), v_ref[pl.ds(ks[g], nloc), :]) + _dot(p_ctx[g].astype(BF16), vc)
          for g in gs]
    for g in gs:
        ol_ref[g * qblk:(g + 1) * qblk, :] = (pv[g] / denom[g]).astype(BF16)


def na_mixer(zc, zl, batch, rpb):
    tc = zc.shape[0] // batch
    tl = zl.shape[0] // batch
    d = zc.shape[1] // 3
    nh = NA_HEADS
    hd = d // nh
    rows = tl // GRID_W
    cases, case_of_block, kstart, nbk = _na_cases(rows)
    bias = na_bias_tables(rpb, cases, nbk)
    qblk = NA_QROWS * GRID_W
    nblk = rows // NA_QROWS
    group = NA_GROUP if nblk % NA_GROUP == 0 else 1
    nstep = nblk // group
    nloc = nbk * GRID_W
    grid_spec = pltpu.PrefetchScalarGridSpec(
        num_scalar_prefetch=2,
        grid=(nh, batch, nstep),
        in_specs=[pl.BlockSpec((group * qblk, hd), lambda h, b, i, *_: (b * nstep + i, h)),
                  pl.BlockSpec((tl, hd), lambda h, b, i, *_: (b, nh + h)),
                  pl.BlockSpec((tl, hd), lambda h, b, i, *_: (b, 2 * nh + h)),
                  pl.BlockSpec((tc, hd), lambda h, b, i, *_: (b, h)),
                  pl.BlockSpec((tc, hd), lambda h, b, i, *_: (b, nh + h)),
                  pl.BlockSpec((tc, hd), lambda h, b, i, *_: (b, 2 * nh + h)),
                  pl.BlockSpec((1, len(cases), qblk, nloc), lambda h, b, i, *_: (h, 0, 0, 0))],
        out_specs=[pl.BlockSpec((group * qblk, hd), lambda h, b, i, *_: (b * nstep + i, h)),
                   pl.BlockSpec((tc, hd), lambda h, b, i, *_: (b, h))])
    ol, oc = pl.pallas_call(
        functools.partial(_na_kernel, nloc=nloc, scale=hd ** -0.5, group=group),
        grid_spec=grid_spec,
        out_shape=[jax.ShapeDtypeStruct((batch * tl, d), BF16),
                   jax.ShapeDtypeStruct((batch * tc, d), BF16)],
        compiler_params=_cparams(("parallel", "parallel", "arbitrary")),
        name="na_mixer",
    )(jnp.asarray(case_of_block, jnp.int32), jnp.asarray(kstart, jnp.int32),
      zl, zl, zl, zc, zc, zc, bias)
    return ol, oc


TM = 512
TM_PROJ = 1024
TN_PROJ = 1024
TN_OUT = 512
TF_MLP = 1024
TN_MLP = 256


def kernel(x, c, ctx, c_ctx, ada_w, ada_b, norm_g, mlp_w1, mlp_w2, ret_w_in, ret_w_out, gdn_w_in, gdn_conv_w, gdn_w_ab, gdn_a_log, gdn_dt_bias, gdn_norm_w, gdn_w_out, na_w_in, na_w_out, na_rpb):
    batch, t, d = x.shape
    tctx = ctx.shape[1]
    depth = ada_w.shape[0]

    nrow = -(-(batch + 1) // 16) * 16
    cvec = jnp.zeros((nrow, d), F32).at[0].set(c_ctx).at[1:batch + 1].set(c)
    mods = ada_modulation(cvec, ada_w, ada_b).reshape(depth, nrow, N_MOD, d)

    xl = x.reshape(batch * t, d)
    xc = ctx.reshape(batch * tctx, d)
    ret_lg, ret_cos, ret_sin = retention_tables(t, ret_w_in.shape[2] // 6 // RET_HEADS)

    i_ret = i_gdn = i_na = 0
    for i in range(depth):
        last = i == depth - 1
        m_ctx = mods[i, 0:1]
        m_lat = mods[i, 1:batch + 1]
        g = norm_g[i]
        kind = i % 3

        def project(w, out_dtype=BF16, tn=TN_PROJ):
            ws = column_slabs(w, min(tn, w.shape[1]))
            zc = norm_mod_matmul(xc, m_ctx, batch * tctx, g[0], ws, 0, 1, out_dtype, TM_PROJ)
            zl = norm_mod_matmul(xl, m_lat, t, g[0], ws, 0, 1, out_dtype, TM_PROJ)
            return zc, zl

        if kind == 0:
            zc, zl = project(ret_w_in[i_ret].astype(BF16))
            al, ac = retention_mixer(zc, zl, batch, ret_lg, ret_cos, ret_sin, not last)
            w_out = ret_w_out[i_ret].astype(BF16)
            i_ret += 1
        elif kind == 1:
            zc, zl = project(gdn_w_in[i_gdn].astype(BF16))
            w_gate, alog_col, dtb_col = gdn_gate_weights(gdn_w_ab[i_gdn], gdn_a_log[i_gdn],
                                                        gdn_dt_bias[i_gdn])
            abc, abl = project(w_gate.astype(BF16), out_dtype=F32)
            gtc = gdn_gates(abc.reshape(batch, tctx, -1).transpose(0, 2, 1), alog_col, dtb_col)
            gtl = gdn_gates(abl.reshape(batch, t, -1).transpose(0, 2, 1), alog_col, dtb_col)
            al, ac = gdn_mixer(zc, zl, gtc, gtl, batch, gdn_conv_w[i_gdn], gdn_norm_w[i_gdn])
            w_out = gdn_w_out[i_gdn].astype(BF16)
            i_gdn += 1
        else:
            zc, zl = project(na_w_in[i_na].astype(BF16))
            al, ac = na_mixer(zc, zl, batch, na_rpb[i_na])
            w_out = na_w_out[i_na].astype(BF16)
            i_na += 1

        w1 = column_slabs(mlp_w1[i].astype(BF16), TF_MLP)
        w2 = mlp_w2[i].astype(BF16)
        w_out = column_slabs(w_out, TN_OUT)
        xl = out_proj_residual(al, w_out, xl, m_lat, t, g[1], 2, TM)
        xl = mlp_residual(xl, m_lat, t, g[2], g[3], w1, w2, TM)
        if not last:
            xc = out_proj_residual(ac, w_out, xc, m_ctx, batch * tctx, g[1], 2, TM)
            xc = mlp_residual(xc, m_ctx, batch * tctx, g[2], g[3], w1, w2, TM)
    return xl.reshape(batch, t, d)
```
